```python
import math
import jax, jax.numpy as jnp
from jax import lax
import numpy as np

D_MODEL = 2048
BATCH = 4
SEQ = 4096
DEPTH = 2

RWKV_HEAD = 64
RWKV_DIM = D_MODEL
RWKV_HEADS = RWKV_DIM // RWKV_HEAD
DECAY_LORA = max(32, int(round(1.8 * RWKV_DIM ** 0.5 / 32)) * 32)
ICLR_LORA = max(32, int(round(1.8 * RWKV_DIM ** 0.5 / 32)) * 32)
VRES_LORA = max(32, int(round(1.3 * RWKV_DIM ** 0.5 / 32)) * 32)
GATE_LORA = max(32, int(round(0.6 * RWKV_DIM ** 0.8 / 32)) * 32)
GN_EPS = 1e-5 * RWKV_HEAD
DIFF_HEAD = 128
DIFF_HEADS = D_MODEL // (2 * DIFF_HEAD)
QK_DIM = DIFF_HEADS * 2 * DIFF_HEAD
V_DIM = DIFF_HEADS * 2 * DIFF_HEAD
Q_BLOCK = 128
ROPE_THETA = 10000.0
SUBLN_EPS = 1e-5
FFN_HIDDEN = int(math.ceil(8 * D_MODEL / 3 / 256)) * 256
RWKV_COLS = 3 * RWKV_DIM + 2 * DECAY_LORA + 2 * ICLR_LORA + GATE_LORA
W_IN_COLS = RWKV_COLS + 2 * QK_DIM + V_DIM + 2 * D_MODEL
DEEPNORM_ALPHA = (2 * DEPTH) ** 0.25
DEEPNORM_BETA = (8 * DEPTH) ** -0.25
LN_EPS = 1e-5

kernel_name = "hybrid_rwkv7_diffattn_deepnorm_adaln_encoder"


def _offsets(sizes):
    out, acc = [], 0
    for s in sizes[:-1]:
        acc += s
        out.append(acc)
    return out


def layer_norm(x, g, b):
    xf = x.astype(jnp.float32)
    mu = jnp.mean(xf, -1, keepdims=True)
    var = jnp.mean(jnp.square(xf - mu), -1, keepdims=True)
    return ((xf - mu) * lax.rsqrt(var + LN_EPS)).astype(x.dtype) * g + b


def centred_token_shift(z, mu_prev, mu_next):
    z_prev = jnp.pad(z[:, :-1], ((0, 0), (1, 0), (0, 0)))
    z_next = jnp.pad(z[:, 1:], ((0, 0), (0, 1), (0, 0)))
    return z + mu_prev * (z_prev - z) + mu_next * (z_next - z)


def rope_tables(seq, dim, dtype):
    pos = jnp.arange(seq, dtype=jnp.float32)
    inv = ROPE_THETA ** (-jnp.arange(0, dim, 2, dtype=jnp.float32) / dim)
    ang = pos[:, None] * inv[None, :]
    emb = jnp.concatenate([ang, ang], axis=-1)
    return jnp.cos(emb).astype(dtype), jnp.sin(emb).astype(dtype)


def apply_rope(t, cos, sin):
    t1, t2 = jnp.split(t, 2, axis=-1)
    rot = jnp.concatenate([-t2, t1], axis=-1)
    c = cos[:, None, None, :]
    s = sin[:, None, None, :]
    return t * c + rot * s


def wkv7_scan(r, decay, k, v, kk, a, reverse):
    B, S, H, N = r.shape

    def step(state, inp):
        r_t, w_t, k_t, v_t, kk_t, a_t = inp
        s_kk = jnp.einsum('bhvk,bhk->bhv', state, kk_t)
        state = (state * w_t[:, :, None, :]
                 - s_kk[..., None] * (kk_t * a_t)[:, :, None, :]
                 + v_t[..., None] * k_t[:, :, None, :])
        return state, jnp.einsum('bhvk,bhk->bhv', state, r_t)

    xs = tuple(jnp.swapaxes(t, 0, 1) for t in (r, decay, k, v, kk, a))
    state0 = jnp.zeros((B, H, N, N), jnp.float32)
    _, out = lax.scan(step, state0, xs, reverse=reverse)
    return jnp.swapaxes(out, 0, 1)


def rwkv7_mixer(z, v_first, v_mix, decay_w0, decay_up, iclr_a0, iclr_up, gate_up,
                k_k, k_a, r_k, ln_w, ln_b):
    B, S, _ = z.shape
    r, k, v, xw_f, xw_b, xa_f, xa_b, xg = jnp.split(
        z, _offsets([RWKV_DIM] * 3 + [DECAY_LORA] * 2 + [ICLR_LORA] * 2 + [GATE_LORA]), axis=-1)
    if v_mix is None:
        v_first = v
    else:
        v = v + (v_first - v) * v_mix

    def heads(t):
        return t.reshape(B, S, RWKV_HEADS, RWKV_HEAD).astype(jnp.float32)

    r_h, v_h = heads(r), heads(v)
    kk_h = heads(k * k_k)
    kk_h = kk_h * lax.rsqrt(jnp.maximum(jnp.sum(kk_h * kk_h, -1, keepdims=True), 1e-24))

    o_sum = None
    k_sum = None
    for d, (xw, xa) in enumerate(((xw_f, xa_f), (xw_b, xa_b))):
        w_log = -jax.nn.softplus(-(decay_w0[d] + jnp.tanh(xw) @ decay_up[d])) - 0.5
        decay = jnp.exp(-jnp.exp(heads(w_log)))
        a = jax.nn.sigmoid(iclr_a0[d] + xa @ iclr_up[d])
        k_d = heads(k * (1 + (a - 1) * k_a))
        o_d = wkv7_scan(r_h, decay, k_d, v_h, kk_h, heads(a), reverse=(d == 1))
        o_sum = o_d if o_sum is None else o_sum + o_d
        k_sum = k_d if k_sum is None else k_sum + k_d

    mu = jnp.mean(o_sum, -1, keepdims=True)
    var = jnp.mean(jnp.square(o_sum - mu), -1, keepdims=True)
    o_n = ((o_sum - mu) * lax.rsqrt(var + GN_EPS)).reshape(B, S, RWKV_DIM)
    o_n = o_n * ln_w.astype(jnp.float32) + ln_b.astype(jnp.float32)
    bonus = (jnp.sum(r_h * k_sum * r_k.astype(jnp.float32), -1, keepdims=True) * v_h).reshape(B, S, RWKV_DIM)
    g = jax.nn.sigmoid(xg) @ gate_up
    y = (o_n + bonus).astype(z.dtype) * g
    return y, v_first


def diff_attention_mixer(zq, zk, zv, cos, sin, lq1, lk1, lq2, lk2, subln_w, lam_init):
    B, S, _ = zq.shape
    q = apply_rope(zq.reshape(B, S, DIFF_HEADS, 2, DIFF_HEAD), cos, sin) * (DIFF_HEAD ** -0.5)
    k = apply_rope(zk.reshape(B, S, DIFF_HEADS, 2, DIFF_HEAD), cos, sin)
    v = zv.reshape(B, S, DIFF_HEADS, 2 * DIFF_HEAD)
    lam = (jnp.exp(jnp.sum(lq1.astype(jnp.float32) * lk1.astype(jnp.float32)))
           - jnp.exp(jnp.sum(lq2.astype(jnp.float32) * lk2.astype(jnp.float32))) + lam_init)
    qb = min(Q_BLOCK, S)
    nb = S // qb
    q_blocks = jnp.moveaxis(q.reshape(B, nb, qb, DIFF_HEADS, 2, DIFF_HEAD), 1, 0)

    def attend(q_blk):
        s = jnp.einsum('bqhmd,bkhmd->bhmqk', q_blk, k).astype(jnp.float32)
        p = jax.nn.softmax(s, axis=-1)
        w = (p[:, :, 0] - lam * p[:, :, 1]).astype(v.dtype)
        return jnp.einsum('bhqk,bkhe->bqhe', w, v)

    o = lax.map(attend, q_blocks)
    o = jnp.moveaxis(o, 0, 1).reshape(B, S, DIFF_HEADS, 2 * DIFF_HEAD).astype(jnp.float32)
    o = o * lax.rsqrt(jnp.mean(o * o, -1, keepdims=True) + SUBLN_EPS) * subln_w.astype(jnp.float32)
    return (o * (1.0 - lam_init)).astype(zq.dtype).reshape(B, S, V_DIM)


def setup_inputs(seed: int = 0) -> dict:
    key = jax.random.key(seed)
    ks = iter(jax.random.split(key, 40))
    f32 = jnp.float32

    def nrm(shape, scale):
        return jax.random.normal(next(ks), shape, f32) * scale

    def unif(shape, lo, hi):
        return jax.random.uniform(next(ks), shape, f32, lo, hi)

    L = DEPTH
    return {
        "x": nrm((BATCH, SEQ, D_MODEL), 1.0),
        "c": nrm((BATCH, D_MODEL), 1.0),
        "ada_w": nrm((L, D_MODEL, 6 * D_MODEL), 0.5 * D_MODEL ** -0.5),
        "ada_b": nrm((L, 6 * D_MODEL), 0.01),
        "w_in": nrm((L, D_MODEL, W_IN_COLS), D_MODEL ** -0.5),
        "shift_mu_prev": unif((L, RWKV_COLS), 0.0, 0.5),
        "shift_mu_next": unif((L, RWKV_COLS), 0.0, 0.5),
        "decay_w0": unif((L, 2, RWKV_DIM), -2.0, 1.0),
        "decay_up": nrm((L, 2, DECAY_LORA, RWKV_DIM), 0.5 * DECAY_LORA ** -0.5),
        "iclr_a0": unif((L, 2, RWKV_DIM), -1.0, 1.0),
        "iclr_up": nrm((L, 2, ICLR_LORA, RWKV_DIM), 0.5 * ICLR_LORA ** -0.5),
        "gate_up": nrm((L, GATE_LORA, RWKV_DIM), GATE_LORA ** -0.5),
        "k_k": 0.85 + nrm((L, RWKV_DIM), 0.05),
        "k_a": 1.0 + nrm((L, RWKV_DIM), 0.05),
        "r_k": nrm((L, RWKV_HEADS, RWKV_HEAD), 0.1),
        "ln_x_w": 1.0 + nrm((L, RWKV_DIM), 0.05),
        "ln_x_b": nrm((L, RWKV_DIM), 0.01),
        "vres_down": nrm((L - 1, D_MODEL, VRES_LORA), D_MODEL ** -0.5),
        "vres_up": nrm((L - 1, VRES_LORA, RWKV_DIM), 0.5 * VRES_LORA ** -0.5),
        "vres_v0": 1.0 + nrm((L - 1, RWKV_DIM), 0.1),
        "lambda_q1": nrm((L, DIFF_HEAD), 0.1),
        "lambda_k1": nrm((L, DIFF_HEAD), 0.1),
        "lambda_q2": nrm((L, DIFF_HEAD), 0.1),
        "lambda_k2": nrm((L, DIFF_HEAD), 0.1),
        "subln_w": 1.0 + nrm((L, 2 * DIFF_HEAD), 0.05),
        "proj_a": nrm((L, RWKV_DIM, D_MODEL), RWKV_DIM ** -0.5),
        "proj_b": nrm((L, V_DIM, D_MODEL), V_DIM ** -0.5),
        "w_out": nrm((L, D_MODEL, D_MODEL), DEEPNORM_BETA * D_MODEL ** -0.5),
        "ln1_g": 1.0 + nrm((L, D_MODEL), 0.05),
        "ln1_b": nrm((L, D_MODEL), 0.01),
        "ffn_w_gate": nrm((L, D_MODEL, FFN_HIDDEN), D_MODEL ** -0.5),
        "ffn_w_up": nrm((L, D_MODEL, FFN_HIDDEN), D_MODEL ** -0.5),
        "ffn_w_down": nrm((L, FFN_HIDDEN, D_MODEL), DEEPNORM_BETA * FFN_HIDDEN ** -0.5),
        "ln2_g": 1.0 + nrm((L, D_MODEL), 0.05),
        "ln2_b": nrm((L, D_MODEL), 0.01),
    }


def reference(x, c, ada_w, ada_b, w_in, shift_mu_prev, shift_mu_next, decay_w0, decay_up,
              iclr_a0, iclr_up, gate_up, k_k, k_a, r_k, ln_x_w, ln_x_b, vres_down, vres_up,
              vres_v0, lambda_q1, lambda_k1, lambda_q2, lambda_k2, subln_w, proj_a, proj_b,
              w_out, ln1_g, ln1_b, ffn_w_gate, ffn_w_up, ffn_w_down, ln2_g, ln2_b):
    B, S, _ = x.shape
    cos, sin = rope_tables(S, DIFF_HEAD, x.dtype)
    c_act = jax.nn.silu(c)
    in_cuts = _offsets([RWKV_COLS, QK_DIM, QK_DIM, V_DIM, D_MODEL, D_MODEL])
    v_first = None
    for l in range(DEPTH):
        mod = (c_act @ ada_w[l] + ada_b[l])[:, None, :]
        sh_m, sc_m, g_m, sh_f, sc_f, g_f = jnp.split(mod, 6, axis=-1)

        u = x * (1 + sc_m) + sh_m
        z = u @ w_in[l]
        z_rwkv, z_q, z_k, z_v, z_ga, z_gb = jnp.split(z, in_cuts, axis=-1)
        z_rwkv = centred_token_shift(z_rwkv, shift_mu_prev[l], shift_mu_next[l])
        v_mix = None if l == 0 else jax.nn.sigmoid(vres_v0[l - 1] + (u @ vres_down[l - 1]) @ vres_up[l - 1])
        y_a, v_first = rwkv7_mixer(z_rwkv, v_first, v_mix, decay_w0[l], decay_up[l], iclr_a0[l],
                                   iclr_up[l], gate_up[l], k_k[l], k_a[l], r_k[l], ln_x_w[l], ln_x_b[l])
        lam_init = 0.8 - 0.6 * math.exp(-0.3 * l)
        y_b = diff_attention_mixer(z_q, z_k, z_v, cos, sin, lambda_q1[l], lambda_k1[l],
                                   lambda_q2[l], lambda_k2[l], subln_w[l], lam_init)
        merged = jax.nn.sigmoid(z_ga) * (y_a @ proj_a[l]) + jax.nn.sigmoid(z_gb) * (y_b @ proj_b[l])
        x = layer_norm(DEEPNORM_ALPHA * x + g_m * (merged @ w_out[l]), ln1_g[l], ln1_b[l])

        u = x * (1 + sc_f) + sh_f
        h = jax.nn.silu(u @ ffn_w_gate[l]) * (u @ ffn_w_up[l])
        x = layer_norm(DEEPNORM_ALPHA * x + g_f * (h @ ffn_w_down[l]), ln2_g[l], ln2_b[l])
    return x
```

```python
import functools
import math

import numpy as np
import jax
import jax.numpy as jnp
from jax import lax
from jax.experimental import pallas as pl
from jax.experimental.pallas import tpu as pltpu

F32 = jnp.float32
BF16 = jnp.bfloat16
HIGHEST = lax.Precision.HIGHEST

RWKV_HEAD = 64
DIFF_HEAD = 128
ROPE_THETA = 10000.0
SUBLN_EPS = 1e-5
LN_EPS = 1e-5
GN_EPS = 1e-5 * RWKV_HEAD
EXP_NEG_HALF = math.exp(-0.5)

LANES = 128
MXU_DIM = 256
VMEM_LIMIT = 56 * 1024 * 1024

CHUNK = 64
HEADS_PER_GROUP = MXU_DIM // RWKV_HEAD
GROUP_LANES = HEADS_PER_GROUP * RWKV_HEAD
STACK_ROWS = HEADS_PER_GROUP * CHUNK
HALO = 16


def _sigmoid(x):
    return 1.0 / (1.0 + jnp.exp(-x))


def _dot(a, b, precision=None):
    return jnp.dot(a, b, preferred_element_type=F32, precision=precision)


def _dot_nt(a, b):
    return lax.dot_general(a, b, (((1,), (1,)), ((), ())), preferred_element_type=F32)


def _dot_tn(a, b):
    return lax.dot_general(a, b, (((0,), (0,)), ((), ())), preferred_element_type=F32)


def _params(*semantics):
    return pltpu.CompilerParams(dimension_semantics=semantics, vmem_limit_bytes=VMEM_LIMIT)


def _pick_tile(n, target, quantum):
    best = None
    t = quantum
    while t <= min(n, target):
        if n % t == 0:
            best = t
        t += quantum
    assert best is not None, (n, target, quantum)
    return best


def _mod_kernel(c_ref, w_ref, b_ref, o_ref):
    c = c_ref[...]
    c_act = c * _sigmoid(c)
    o_ref[...] = _dot(c_act, w_ref[...], HIGHEST) + b_ref[...]


def _modulation(c, ada_w, ada_b):
    nl, d, n6 = ada_w.shape
    b = c.shape[0]
    rows = -(-b // 8) * 8
    c_pad = jnp.pad(c, ((0, rows - b), (0, 0)))
    tn = _pick_tile(n6, 512, LANES)
    out = pl.pallas_call(
        _mod_kernel,
        out_shape=jax.ShapeDtypeStruct((nl, rows, n6), F32),
        grid=(nl, n6 // tn),
        in_specs=[
            pl.BlockSpec((rows, d), lambda l, j: (0, 0)),
            pl.BlockSpec((None, d, tn), lambda l, j: (l, 0, j)),
            pl.BlockSpec((None, 1, tn), lambda l, j: (l, 0, j)),
        ],
        out_specs=pl.BlockSpec((None, rows, tn), lambda l, j: (l, 0, j)),
        compiler_params=_params("parallel", "parallel"),
        name="adaln_mod",
    )(c_pad, ada_w, ada_b.reshape(nl, 1, n6))
    return out[:, :b].reshape(nl, b, 6, d)


def _inproj_kernel(x_ref, xp_ref, xn_ref, mod_ref, w_ref, mu_ref, o_ref, u_ref, acc_ref, *,
                   tm, tiles_per_seq, shift_row, scale_row):
    i = pl.program_id(0)
    j = pl.program_id(1)

    @pl.when(j == 0)
    def _():
        sc = 1.0 + mod_ref[scale_row:scale_row + 1, :]
        sh = mod_ref[shift_row:shift_row + 1, :]
        u_ref[0:HALO, :] = (xp_ref[...] * sc + sh).astype(BF16)
        u_ref[HALO:HALO + tm, :] = (x_ref[...] * sc + sh).astype(BF16)
        u_ref[HALO + tm:, :] = (xn_ref[...] * sc + sh).astype(BF16)

    acc_ref[...] = _dot(u_ref[...], w_ref[...])
    zc = acc_ref[HALO:HALO + tm, :]
    zp = acc_ref[HALO - 1:HALO - 1 + tm, :]
    zn = acc_ref[HALO + 1:HALO + 1 + tm, :]
    row = lax.broadcasted_iota(jnp.int32, (tm, 1), 0)
    pos = i % tiles_per_seq
    zp = jnp.where(jnp.logical_and(row == 0, pos == 0), 0.0, zp)
    zn = jnp.where(jnp.logical_and(row == tm - 1, pos == tiles_per_seq - 1), 0.0, zn)
    o_ref[...] = (zc + mu_ref[0:1, :] * (zp - zc) + mu_ref[1:2, :] * (zn - zc)).astype(o_ref.dtype)


def _plain_inproj_kernel(x_ref, mod_ref, w_ref, o_ref, u_ref, *, shift_row, scale_row):
    @pl.when(pl.program_id(1) == 0)
    def _():
        sc = 1.0 + mod_ref[scale_row:scale_row + 1, :]
        sh = mod_ref[shift_row:shift_row + 1, :]
        u_ref[...] = (x_ref[...] * sc + sh).astype(BF16)

    o_ref[...] = _dot(u_ref[...], w_ref[...]).astype(o_ref.dtype)


def _inproj_shift(x2, mod_l, w, mu, seq, out_dtype):
    t, d = x2.shape
    n = w.shape[1]
    tm = _pick_tile(seq, 1024, HALO)
    tn = _pick_tile(n, 512, LANES)
    tps = seq // tm
    hb = tm // HALO
    last_halo = t // HALO - 1
    kern = functools.partial(_inproj_kernel, tm=tm, tiles_per_seq=tps, shift_row=0, scale_row=1)
    return pl.pallas_call(
        kern,
        out_shape=jax.ShapeDtypeStruct((t, n), out_dtype),
        grid=(t // tm, n // tn),
        in_specs=[
            pl.BlockSpec((tm, d), lambda i, j: (i, 0)),
            pl.BlockSpec((HALO, d), lambda i, j: (jnp.maximum(i * hb - 1, 0), 0)),
            pl.BlockSpec((HALO, d), lambda i, j: (jnp.minimum((i + 1) * hb, last_halo), 0)),
            pl.BlockSpec((None, 6, d), lambda i, j: (i // tps, 0, 0)),
            pl.BlockSpec((d, tn), lambda i, j: (0, j)),
            pl.BlockSpec((2, tn), lambda i, j: (0, j)),
        ],
        out_specs=pl.BlockSpec((tm, tn), lambda i, j: (i, j)),
        scratch_shapes=[pltpu.VMEM((tm + 2 * HALO, d), BF16), pltpu.VMEM((tm + 2 * HALO, tn), F32)],
        compiler_params=_params("parallel", "arbitrary"),
        name="inproj_shift",
    )(x2, x2, x2, mod_l, w, mu)


def _inproj_plain(x2, mod_l, w, seq, out_dtype):
    t, d = x2.shape
    n = w.shape[1]
    tm = _pick_tile(seq, 1024, HALO)
    tn = _pick_tile(n, 512, LANES)
    tps = seq // tm
    kern = functools.partial(_plain_inproj_kernel, shift_row=0, scale_row=1)
    return pl.pallas_call(
        kern,
        out_shape=jax.ShapeDtypeStruct((t, n), out_dtype),
        grid=(t // tm, n // tn),
        in_specs=[
            pl.BlockSpec((tm, d), lambda i, j: (i, 0)),
            pl.BlockSpec((None, 6, d), lambda i, j: (i // tps, 0, 0)),
            pl.BlockSpec((d, tn), lambda i, j: (0, j)),
        ],
        out_specs=pl.BlockSpec((tm, tn), lambda i, j: (i, j)),
        scratch_shapes=[pltpu.VMEM((tm, d), BF16)],
        compiler_params=_params("parallel", "arbitrary"),
        name="inproj_plain",
    )(x2, mod_l, w)


def _wkv_constants():
    c, g, gc, gl = CHUNK, HEADS_PER_GROUP, STACK_ROWS, GROUP_LANES
    t = np.arange(c)
    tri_f = (t[None, :] <= t[:, None]).astype(np.float32)
    tri = np.stack([tri_f, tri_f.T])
    rh, rt = np.arange(gc) // c, np.arange(gc) % c
    same = rh[:, None] == rh[None, :]
    strict_f = same & (rt[None, :] < rt[:, None])
    incl_f = same & (rt[None, :] <= rt[:, None])
    strict = np.stack([strict_f, same & (rt[None, :] > rt[:, None])]).astype(np.float32)
    incl = np.stack([incl_f, same & (rt[None, :] >= rt[:, None])]).astype(np.float32)
    lane_h = np.arange(gl) // RWKV_HEAD
    stack_mask = (rh[:, None] == lane_h[None, :]).astype(np.float32)
    block_diag = (lane_h[:, None] == lane_h[None, :]).astype(np.float32)
    eye = np.eye(gc, dtype=np.float32)
    return tri, strict, incl, stack_mask, block_diag, eye


def _wkv_direction(d, zr, zk, zv, xw, xa, v0, vr, prm, cst, q_ref, o_ref, bon_ref, has_vres):
    dup, w0, iup, a0, kkr, vup, vv0 = prm
    tri, strict, incl, smask, bdiag, eye = cst
    c, g = CHUNK, HEADS_PER_GROUP

    r = zr[...]
    k = zk[...]
    v = zv[...]
    if has_vres:
        mix = _sigmoid(vv0[...] + _dot(vr[...].astype(BF16), vup[...]))
        v = v + (v0[...] - v) * mix
    k_k = kkr[0:1, :]
    k_a = kkr[1:2, :]
    r_k = kkr[2:3, :]
    bd = bdiag[...]

    lw = -EXP_NEG_HALF * _sigmoid(w0[d:d + 1, :] + _dot(jnp.tanh(xw[...]).astype(BF16), dup[d]))
    a = _sigmoid(a0[d:d + 1, :] + _dot(xa[...].astype(BF16), iup[d]))
    kd = k * (1.0 + (a - 1.0) * k_a)
    kk0 = k * k_k
    kk = kk0 * lax.rsqrt(jnp.maximum(_dot(kk0 * kk0, bd, HIGHEST), 1e-24))
    bon_ref[...] = _dot(r * kd * r_k, bd, HIGHEST) * v

    cum = _dot(tri[d], lw, HIGHEST)
    tot = jnp.sum(lw, axis=0, keepdims=True)
    g_inv = jnp.exp(-cum)
    b = kk * a
    a_t = -(kk * jnp.exp(cum - lw))
    b_t = b * g_inv
    k_t = kd * g_inv
    r_t = r * jnp.exp(cum)
    g_rem = jnp.exp(tot - cum)
    b_g = b * g_rem
    k_g = kd * g_rem

    smb = smask[...] > 0.5

    def stack(x):
        return jnp.where(smb, jnp.concatenate([x] * g, axis=0), 0.0).astype(BF16)

    def tile(x):
        return jnp.concatenate([x] * g, axis=0).astype(BF16)

    def unstack(xs):
        out = xs[0:c]
        for h in range(1, g):
            out = out + xs[h * c:(h + 1) * c]
        return out

    a_s, r_s, v_s = stack(a_t), stack(r_t), stack(v)
    b_tl, k_tl = tile(b_t), tile(k_t)
    ms = strict[d] > 0.5
    mi = incl[d] > 0.5
    n_ab = jnp.where(ms, _dot_nt(a_s, b_tl), 0.0)
    n_ak = jnp.where(ms, _dot_nt(a_s, k_tl), 0.0).astype(BF16)
    n_rb = jnp.where(mi, _dot_nt(r_s, b_tl), 0.0).astype(BF16)
    n_rk = jnp.where(mi, _dot_nt(r_s, k_tl), 0.0).astype(BF16)

    inv = eye[...] + n_ab
    pw = n_ab.astype(BF16)
    steps = int(math.log2(c)) - 1
    for s in range(steps):
        pw_f = _dot(pw, pw)
        pw = pw_f.astype(BF16)
        inv = inv + _dot(inv.astype(BF16), pw)
    inv = inv.astype(BF16)

    q = q_ref[...]
    qb = q.astype(BF16)
    y = _dot_nt(a_s, qb) + _dot(n_ak, v_s)
    u_s = _dot(inv, y.astype(BF16))
    o_s = _dot_nt(r_s, qb) + _dot(n_rb, u_s.astype(BF16)) + _dot(n_rk, v_s)
    o_ref[...] = unstack(o_s)
    u = unstack(u_s)
    upd = _dot_tn(jnp.concatenate([u, v], axis=0).astype(BF16),
                  jnp.concatenate([b_g, k_g], axis=0).astype(BF16))
    q_ref[...] = jnp.where(bd > 0.5, q * jnp.exp(tot) + upd, 0.0)


def _wkv_kernel(*refs, has_vres):
    n_dir = 7 if has_vres else 5
    fwd = refs[0:n_dir]
    bwd = refs[n_dir:2 * n_dir]
    p0 = 2 * n_dir
    n_prm = 7 if has_vres else 5
    prm = list(refs[p0:p0 + n_prm])
    if not has_vres:
        prm += [None, None]
    c0 = p0 + n_prm
    cst = refs[c0:c0 + 6]
    o_f, o_b, bon_f, bon_b = refs[c0 + 6:c0 + 10]
    q_f, q_b = refs[c0 + 10:c0 + 12]

    @pl.when(pl.program_id(2) == 0)
    def _():
        q_f[...] = jnp.zeros_like(q_f)
        q_b[...] = jnp.zeros_like(q_b)

    def unpack(rs):
        zr, zk, zv, xw, xa = rs[0:5]
        v0, vr = (rs[5], rs[6]) if has_vres else (None, None)
        return zr, zk, zv, xw, xa, v0, vr

    _wkv_direction(0, *unpack(fwd), prm, cst, q_f, o_f, bon_f, has_vres)
    _wkv_direction(1, *unpack(bwd), prm, cst, q_b, o_b, bon_b, has_vres)


def _wkv(zs, zs_first, prm_arrays, batch, seq, d):
    t = zs.shape[0]
    has_vres = zs_first is not None
    c, gl = CHUNK, GROUP_LANES
    assert seq % c == 0 and d % gl == 0
    nc = seq // c
    ng = d // gl
    lora0 = 3 * d // LANES

    def rows(rev):
        if rev:
            return lambda b, g, i: b * nc + (nc - 1 - i)
        return lambda b, g, i: b * nc + i

    def dir_specs(rev):
        rw = rows(rev)
        specs = [
            pl.BlockSpec((c, gl), lambda b, g, i: (rw(b, g, i), g)),
            pl.BlockSpec((c, gl), lambda b, g, i: (rw(b, g, i), ng + g)),
            pl.BlockSpec((c, gl), lambda b, g, i: (rw(b, g, i), 2 * ng + g)),
            pl.BlockSpec((c, LANES), lambda b, g, i: (rw(b, g, i), lora0 + rev)),
            pl.BlockSpec((c, LANES), lambda b, g, i: (rw(b, g, i), lora0 + 2 + rev)),
        ]
        args = [zs] * 5
        if has_vres:
            specs += [
                pl.BlockSpec((c, gl), lambda b, g, i: (rw(b, g, i), 2 * ng + g)),
                pl.BlockSpec((c, LANES), lambda b, g, i: (rw(b, g, i), lora0 + 6)),
            ]
            args += [zs_first, zs]
        return specs, args

    fs, fa = dir_specs(0)
    bs, ba = dir_specs(1)
    dup, w0, iup, a0, kkr, vup, vv0 = prm_arrays
    p_specs = [
        pl.BlockSpec((2, LANES, gl), lambda b, g, i: (0, 0, g)),
        pl.BlockSpec((2, gl), lambda b, g, i: (0, g)),
        pl.BlockSpec((2, LANES, gl), lambda b, g, i: (0, 0, g)),
        pl.BlockSpec((2, gl), lambda b, g, i: (0, g)),
        pl.BlockSpec((3, gl), lambda b, g, i: (0, g)),
    ]
    p_args = [dup, w0, iup, a0, kkr]
    if has_vres:
        p_specs += [
            pl.BlockSpec((LANES, gl), lambda b, g, i: (0, g)),
            pl.BlockSpec((1, gl), lambda b, g, i: (0, g)),
        ]
        p_args += [vup, vv0]
    consts = [jnp.asarray(x) for x in _wkv_constants()]
    c_specs = [pl.BlockSpec(x.shape, lambda b, g, i, nd=x.ndim: (0,) * nd) for x in consts]

    out_f = pl.BlockSpec((c, gl), lambda b, g, i: (b * nc + i, g))
    out_b = pl.BlockSpec((c, gl), lambda b, g, i: (b * nc + (nc - 1 - i), g))
    shp = jax.ShapeDtypeStruct((t, d), F32)
    return pl.pallas_call(
        functools.partial(_wkv_kernel, has_vres=has_vres),
        out_shape=[shp, shp, shp, shp],
        grid=(batch, ng, nc),
        in_specs=fs + bs + p_specs + c_specs,
        out_specs=[out_f, out_b, out_f, out_b],
        scratch_shapes=[pltpu.VMEM((gl, gl), F32), pltpu.VMEM((gl, gl), F32)],
        compiler_params=_params("parallel", "parallel", "arbitrary"),
        name="wkv7_chunked",
    )(*fa, *ba, *p_args, *consts)


def _rwkv_out_kernel(of_ref, ob_ref, bf_ref, bb_ref, xg_ref, gup_ref, lnw_ref, lnb_ref, bd_ref, o_ref):
    o = of_ref[...] + ob_ref[...]
    bd = bd_ref[...]
    inv_n = 1.0 / RWKV_HEAD
    mu = _dot(o, bd, HIGHEST) * inv_n
    oc = o - mu
    var = _dot(oc * oc, bd, HIGHEST) * inv_n
    o_n = oc * lax.rsqrt(var + GN_EPS) * lnw_ref[...] + lnb_ref[...]
    gate = _dot(_sigmoid(xg_ref[...]).astype(BF16), gup_ref[...])
    o_ref[...] = ((o_n + bf_ref[...] + bb_ref[...]) * gate).astype(o_ref.dtype)


def _rwkv_out(o_f, o_b, bon_f, bon_b, zs, gate_up, ln_w, ln_b, d):
    t = o_f.shape[0]
    gl = GROUP_LANES
    tm = _pick_tile(t, 1024, 8)
    xg_block = (3 * d + 4 * LANES) // gl
    bd = jnp.asarray(_wkv_constants()[4])
    big = pl.BlockSpec((tm, gl), lambda i, j: (i, j))
    return pl.pallas_call(
        _rwkv_out_kernel,
        out_shape=jax.ShapeDtypeStruct((t, d), BF16),
        grid=(t // tm, d // gl),
        in_specs=[
            big, big, big, big,
            pl.BlockSpec((tm, gl), lambda i, j: (i, xg_block)),
            pl.BlockSpec((gl, gl), lambda i, j: (0, j)),
            pl.BlockSpec((1, gl), lambda i, j: (0, j)),
            pl.BlockSpec((1, gl), lambda i, j: (0, j)),
            pl.BlockSpec((gl, gl), lambda i, j: (0, 0)),
        ],
        out_specs=big,
        compiler_params=_params("parallel", "parallel"),
        name="rwkv_out",
    )(o_f, o_b, bon_f, bon_b, zs, gate_up, ln_w, ln_b, bd)


def _attn_kernel(q_ref, k_ref, v_ref, cos_ref, sin_ref, lam_ref, sw_ref, o_ref, kr_ref, *, tq, lam_init):
    qi = pl.program_id(2)
    dh = DIFF_HEAD

    @pl.when(qi == 0)
    def _():
        cos = cos_ref[...]
        sin = sin_ref[...]
        for m in range(2):
            km = k_ref[:, m * dh:(m + 1) * dh].astype(F32)
            kr_ref[:, m * dh:(m + 1) * dh] = (km * cos + pltpu.roll(km, dh // 2, 1) * sin).astype(BF16)

    lam = (jnp.exp(jnp.sum(lam_ref[0:1, :] * lam_ref[1:2, :], axis=-1, keepdims=True))
           - jnp.exp(jnp.sum(lam_ref[2:3, :] * lam_ref[3:4, :], axis=-1, keepdims=True)) + lam_init)
    row0 = pl.multiple_of(qi * tq, tq)
    cq = cos_ref[pl.ds(row0, tq), :]
    sq = sin_ref[pl.ds(row0, tq), :]
    probs = []
    for m in range(2):
        qm = q_ref[:, m * dh:(m + 1) * dh].astype(F32)
        qm = (qm * cq + pltpu.roll(qm, dh // 2, 1) * sq) * (dh ** -0.5)
        s = _dot_nt(qm.astype(BF16), kr_ref[:, m * dh:(m + 1) * dh])
        p = jnp.exp(s - jnp.max(s, axis=-1, keepdims=True))
        probs.append((p, jnp.sum(p, axis=-1, keepdims=True)))
    (p1, l1), (p2, l2) = probs
    w = p1 * (1.0 / l1) - p2 * (lam / l2)
    o = _dot(w.astype(BF16), v_ref[...])
    o = o * lax.rsqrt(jnp.mean(o * o, axis=-1, keepdims=True) + SUBLN_EPS) * sw_ref[...]
    o_ref[...] = (o * (1.0 - lam_init)).astype(o_ref.dtype)


def _diff_attention(z_att, cos, sin_signed, lam_vecs, subln_w, batch, seq, d, lam_init):
    t = z_att.shape[0]
    hw = 2 * DIFF_HEAD
    nh = d // hw
    tq = _pick_tile(seq, 256, 8)
    nq = seq // tq
    kern = functools.partial(_attn_kernel, tq=tq, lam_init=lam_init)
    return pl.pallas_call(
        kern,
        out_shape=jax.ShapeDtypeStruct((t, d), BF16),
        grid=(batch, nh, nq),
        in_specs=[
            pl.BlockSpec((tq, hw), lambda b, h, i: (b * nq + i, h)),
            pl.BlockSpec((seq, hw), lambda b, h, i: (b, nh + h)),
            pl.BlockSpec((seq, hw), lambda b, h, i: (b, 2 * nh + h)),
            pl.BlockSpec((seq, DIFF_HEAD), lambda b, h, i: (0, 0)),
            pl.BlockSpec((seq, DIFF_HEAD), lambda b, h, i: (0, 0)),
            pl.BlockSpec((4, DIFF_HEAD), lambda b, h, i: (0, 0)),
            pl.BlockSpec((1, hw), lambda b, h, i: (0, 0)),
        ],
        out_specs=pl.BlockSpec((tq, hw), lambda b, h, i: (b * nq + i, h)),
        scratch_shapes=[pltpu.VMEM((seq, hw), BF16)],
        compiler_params=_params("parallel", "parallel", "arbitrary"),
        name="diff_attention",
    )(z_att, z_att, z_att, cos, sin_signed, lam_vecs, subln_w)


def _merge_kernel(ya_ref, yb_ref, pa_ref, pb_ref, ga_ref, gb_ref, o_ref):
    ma = _dot(ya_ref[...], pa_ref[...])
    mb = _dot(yb_ref[...], pb_ref[...])
    ga = _sigmoid(ga_ref[...].astype(F32))
    gb = _sigmoid(gb_ref[...].astype(F32))
    o_ref[...] = (ga * ma + gb * mb).astype(o_ref.dtype)


def _merge(y_a, y_b, proj_a, proj_b, z_att, d):
    t = y_a.shape[0]
    tm = _pick_tile(t, 1024, 8)
    tn = _pick_tile(d, 512, LANES)
    nb = d // tn
    return pl.pallas_call(
        _merge_kernel,
        out_shape=jax.ShapeDtypeStruct((t, d), BF16),
        grid=(t // tm, nb),
        in_specs=[
            pl.BlockSpec((tm, d), lambda i, j: (i, 0)),
            pl.BlockSpec((tm, d), lambda i, j: (i, 0)),
            pl.BlockSpec((d, tn), lambda i, j: (0, j)),
            pl.BlockSpec((d, tn), lambda i, j: (0, j)),
            pl.BlockSpec((tm, tn), lambda i, j: (i, 3 * nb + j)),
            pl.BlockSpec((tm, tn), lambda i, j: (i, 4 * nb + j)),
        ],
        out_specs=pl.BlockSpec((tm, tn), lambda i, j: (i, j)),
        compiler_params=_params("parallel", "arbitrary"),
        name="gated_merge",
    )(y_a, y_b, proj_a, proj_b, z_att, z_att)


def _mm_ln_kernel(a_ref, w_ref, x_ref, mod_ref, g_ref, b_ref, o_ref, acc_ref, *, nk, gate_row, alpha):
    kk = pl.program_id(1)

    @pl.when(kk == 0)
    def _():
        acc_ref[...] = jnp.zeros_like(acc_ref)

    acc_ref[...] += _dot(a_ref[...], w_ref[...])

    @pl.when(kk == nk - 1)
    def _():
        y = alpha * x_ref[...] + mod_ref[gate_row:gate_row + 1, :] * acc_ref[...]
        mu = jnp.mean(y, axis=-1, keepdims=True)
        yc = y - mu
        var = jnp.mean(yc * yc, axis=-1, keepdims=True)
        o_ref[...] = yc * lax.rsqrt(var + LN_EPS) * g_ref[...] + b_ref[...]


def _mm_residual_ln(a, w, x2, mod_l, gate_row, ln_g, ln_b, seq, alpha):
    t, kdim = a.shape
    d = w.shape[1]
    tm = _pick_tile(seq, 512, 8)
    tk = _pick_tile(kdim, 1536, LANES)
    nk = kdim // tk
    tps = seq // tm
    kern = functools.partial(_mm_ln_kernel, nk=nk, gate_row=gate_row, alpha=alpha)
    return pl.pallas_call(
        kern,
        out_shape=jax.ShapeDtypeStruct((t, d), F32),
        grid=(t // tm, nk),
        in_specs=[
            pl.BlockSpec((tm, tk), lambda i, k: (i, k)),
            pl.BlockSpec((tk, d), lambda i, k: (k, 0)),
            pl.BlockSpec((tm, d), lambda i, k: (i, 0)),
            pl.BlockSpec((None, 6, d), lambda i, k: (i // tps, 0, 0)),
            pl.BlockSpec((1, d), lambda i, k: (0, 0)),
            pl.BlockSpec((1, d), lambda i, k: (0, 0)),
        ],
        out_specs=pl.BlockSpec((tm, d), lambda i, k: (i, 0)),
        scratch_shapes=[pltpu.VMEM((tm, d), F32)],
        compiler_params=_params("parallel", "arbitrary"),
        name="matmul_residual_ln",
    )(a, w, x2, mod_l, ln_g, ln_b)


def _ffn_up_kernel(x_ref, mod_ref, wg_ref, wu_ref, o_ref, u_ref, *, shift_row, scale_row):
    @pl.when(pl.program_id(1) == 0)
    def _():
        sc = 1.0 + mod_ref[scale_row:scale_row + 1, :]
        sh = mod_ref[shift_row:shift_row + 1, :]
        u_ref[...] = (x_ref[...] * sc + sh).astype(BF16)

    u = u_ref[...]
    hg = _dot(u, wg_ref[...])
    hu = _dot(u, wu_ref[...])
    o_ref[...] = (hg * _sigmoid(hg) * hu).astype(o_ref.dtype)


def _ffn_up(x2, mod_l, w_gate, w_up, seq):
    t, d = x2.shape
    f = w_gate.shape[1]
    tm = _pick_tile(seq, 1024, 8)
    tn = _pick_tile(f, 512, LANES)
    tps = seq // tm
    kern = functools.partial(_ffn_up_kernel, shift_row=3, scale_row=4)
    return pl.pallas_call(
        kern,
        out_shape=jax.ShapeDtypeStruct((t, f), BF16),
        grid=(t // tm, f // tn),
        in_specs=[
            pl.BlockSpec((tm, d), lambda i, j: (i, 0)),
            pl.BlockSpec((None, 6, d), lambda i, j: (i // tps, 0, 0)),
            pl.BlockSpec((d, tn), lambda i, j: (0, j)),
            pl.BlockSpec((d, tn), lambda i, j: (0, j)),
        ],
        out_specs=pl.BlockSpec((tm, tn), lambda i, j: (i, j)),
        scratch_shapes=[pltpu.VMEM((tm, d), BF16)],
        compiler_params=_params("parallel", "arbitrary"),
        name="ffn_up",
    )(x2, mod_l, w_gate, w_up)


def _pad_cols(a, width):
    return jnp.pad(a, [(0, 0)] * (a.ndim - 1) + [(0, width - a.shape[-1])])


def _pad_rows(a, height):
    return jnp.pad(a, [(0, 0)] * (a.ndim - 2) + [(0, height - a.shape[-2]), (0, 0)])


def _rope_tables(seq):
    pos = jnp.arange(seq, dtype=F32)
    inv = ROPE_THETA ** (-jnp.arange(0, DIFF_HEAD, 2, dtype=F32) / DIFF_HEAD)
    ang = pos[:, None] * inv[None, :]
    emb = jnp.concatenate([ang, ang], axis=-1)
    sign = jnp.where(jnp.arange(DIFF_HEAD) < DIFF_HEAD // 2, -1.0, 1.0).astype(F32)
    return jnp.cos(emb), jnp.sin(emb) * sign[None, :]


def kernel(x, c, ada_w, ada_b, w_in, shift_mu_prev, shift_mu_next, decay_w0, decay_up, iclr_a0, iclr_up, gate_up, k_k, k_a, r_k, ln_x_w, ln_x_b, vres_down, vres_up, vres_v0, lambda_q1, lambda_k1, lambda_q2, lambda_k2, subln_w, proj_a, proj_b, w_out, ln1_g, ln1_b, ffn_w_gate, ffn_w_up, ffn_w_down, ln2_g, ln2_b):
    batch, seq, d = x.shape
    depth = ada_w.shape[0]
    t = batch * seq
    dl, il, gl_rank = decay_up.shape[2], iclr_up.shape[2], gate_up.shape[1]
    vl = vres_down.shape[2]
    assert max(dl, il, vl) <= LANES and gl_rank <= GROUP_LANES
    rwkv_cols = 3 * d + 2 * dl + 2 * il + gl_rank
    alpha = (2 * depth) ** 0.25

    mod = _modulation(c, ada_w, ada_b)
    cos, sin_signed = _rope_tables(seq)
    x2 = x.reshape(t, d)
    zs_first = None

    for l in range(depth):
        mod_l = mod[l]
        w_l = w_in[l]
        cuts = np.cumsum([3 * d, dl, dl, il, il, gl_rank])
        widths = [3 * d, LANES, LANES, LANES, LANES, GROUP_LANES]

        def regroup(a):
            parts = jnp.split(a[..., :rwkv_cols], cuts[:-1], axis=-1)
            return [_pad_cols(p, w) for p, w in zip(parts, widths)]

        w_parts = regroup(w_l)
        if l > 0:
            w_parts.append(_pad_cols(vres_down[l - 1], LANES))
        else:
            w_parts.append(jnp.zeros((d, LANES), F32))
        n_rwkv = sum(p.shape[1] for p in w_parts)
        n_pad = -(-n_rwkv // 512) * 512
        w_rwkv = _pad_cols(jnp.concatenate(w_parts, axis=1), n_pad).astype(BF16)
        mu = jnp.stack([
            _pad_cols(jnp.concatenate(regroup(shift_mu_prev[l]), axis=0), n_pad),
            _pad_cols(jnp.concatenate(regroup(shift_mu_next[l]), axis=0), n_pad)])
        w_att = w_l[:, rwkv_cols:].astype(BF16)

        zs = _inproj_shift(x2, mod_l, w_rwkv, mu, seq, F32)
        z_att = _inproj_plain(x2, mod_l, w_att, seq, BF16)

        prm = [
            _pad_rows(decay_up[l], LANES).astype(BF16),
            decay_w0[l],
            _pad_rows(iclr_up[l], LANES).astype(BF16),
            iclr_a0[l],
            jnp.stack([k_k[l], k_a[l], r_k[l].reshape(d)]),
            _pad_rows(vres_up[l - 1], LANES).astype(BF16) if l > 0 else None,
            vres_v0[l - 1].reshape(1, d) if l > 0 else None,
        ]
        o_f, o_b, bon_f, bon_b = _wkv(zs, zs_first if l > 0 else None, prm, batch, seq, d)
        if l == 0:
            zs_first = zs
        y_a = _rwkv_out(o_f, o_b, bon_f, bon_b, zs, _pad_rows(gate_up[l], GROUP_LANES).astype(BF16),
                        ln_x_w[l].reshape(1, d), ln_x_b[l].reshape(1, d), d)

        lam_init = 0.8 - 0.6 * math.exp(-0.3 * l)
        lam_vecs = jnp.stack([lambda_q1[l], lambda_k1[l], lambda_q2[l], lambda_k2[l]])
        y_b = _diff_attention(z_att, cos, sin_signed, lam_vecs, subln_w[l].reshape(1, -1),
                              batch, seq, d, lam_init)

        merged = _merge(y_a, y_b, proj_a[l].astype(BF16), proj_b[l].astype(BF16), z_att, d)
        x2 = _mm_residual_ln(merged, w_out[l].astype(BF16), x2, mod_l, 2,
                             ln1_g[l].reshape(1, d), ln1_b[l].reshape(1, d), seq, alpha)

        h = _ffn_up(x2, mod_l, ffn_w_gate[l].astype(BF16), ffn_w_up[l].astype(BF16), seq)
        x2 = _mm_residual_ln(h, ffn_w_down[l].astype(BF16), x2, mod_l, 5,
                             ln2_g[l].reshape(1, d), ln2_b[l].reshape(1, d), seq, alpha)

    return x2.reshape(batch, seq, d)
```

```python
import functools
import math

import numpy as np
import jax
import jax.numpy as jnp
from jax import lax
from jax.experimental import pallas as pl
from jax.experimental.pallas import tpu as pltpu

F32 = jnp.float32
BF16 = jnp.bfloat16
HIGHEST = lax.Precision.HIGHEST

RWKV_HEAD = 64
DIFF_HEAD = 128
ROPE_THETA = 10000.0
SUBLN_EPS = 1e-5
LN_EPS = 1e-5
GN_EPS = 1e-5 * RWKV_HEAD
EXP_NEG_HALF = math.exp(-0.5)

LANES = 128
MXU_DIM = 256
VMEM_LIMIT = 56 * 1024 * 1024

CHUNK = 64
HEADS_PER_GROUP = MXU_DIM // RWKV_HEAD
GROUP_LANES = HEADS_PER_GROUP * RWKV_HEAD
STACK_ROWS = HEADS_PER_GROUP * CHUNK
GROUPS_PER_STEP = 2
HALO = 16


def _sigmoid(x):
    return 1.0 / (1.0 + jnp.exp(-x))


def _dot(a, b, precision=None):
    return jnp.dot(a, b, preferred_element_type=F32, precision=precision)


def _dot_nt(a, b):
    return lax.dot_general(a, b, (((1,), (1,)), ((), ())), preferred_element_type=F32)


def _dot_tn(a, b):
    return lax.dot_general(a, b, (((0,), (0,)), ((), ())), preferred_element_type=F32)


def _split2(x):
    hi = x.astype(BF16)
    return hi, (x - hi.astype(F32)).astype(BF16)


def _split3(x):
    hi = x.astype(BF16)
    rest = x - hi.astype(F32)
    mid = rest.astype(BF16)
    return hi, mid, (rest - mid.astype(F32)).astype(BF16)


def _params(*semantics):
    return pltpu.CompilerParams(dimension_semantics=semantics, vmem_limit_bytes=VMEM_LIMIT)


def _pick_tile(n, target, quantum):
    best = None
    t = quantum
    while t <= min(n, target):
        if n % t == 0:
            best = t
        t += quantum
    assert best is not None, (n, target, quantum)
    return best


def _mod_kernel(c_ref, w_ref, b_ref, o_ref):
    c = c_ref[...]
    c_act = c * _sigmoid(c)
    o_ref[...] = _dot(c_act, w_ref[...], HIGHEST) + b_ref[...]


def _modulation(c, ada_w, ada_b):
    nl, d, n6 = ada_w.shape
    b = c.shape[0]
    rows = -(-b // 8) * 8
    c_pad = jnp.pad(c, ((0, rows - b), (0, 0)))
    tn = _pick_tile(n6, 512, LANES)
    out = pl.pallas_call(
        _mod_kernel,
        out_shape=jax.ShapeDtypeStruct((nl, rows, n6), F32),
        grid=(nl, n6 // tn),
        in_specs=[
            pl.BlockSpec((rows, d), lambda l, j: (0, 0)),
            pl.BlockSpec((None, d, tn), lambda l, j: (l, 0, j)),
            pl.BlockSpec((None, 1, tn), lambda l, j: (l, 0, j)),
        ],
        out_specs=pl.BlockSpec((None, rows, tn), lambda l, j: (l, 0, j)),
        compiler_params=_params("parallel", "parallel"),
        name="adaln_mod",
    )(c_pad, ada_w, ada_b.reshape(nl, 1, n6))
    return out[:, :b].reshape(nl, b, 6, d)


def _inproj_kernel(x_ref, xp_ref, xn_ref, mod_ref, w_ref, mu_ref, o_ref, u_ref, acc_ref, *,
                   tm, tiles_per_seq, shift_row, scale_row):
    i = pl.program_id(0)
    j = pl.program_id(1)

    @pl.when(j == 0)
    def _():
        sc = 1.0 + mod_ref[scale_row:scale_row + 1, :]
        sh = mod_ref[shift_row:shift_row + 1, :]
        u_ref[0:HALO, :] = (xp_ref[...] * sc + sh).astype(BF16)
        u_ref[HALO:HALO + tm, :] = (x_ref[...] * sc + sh).astype(BF16)
        u_ref[HALO + tm:, :] = (xn_ref[...] * sc + sh).astype(BF16)

    acc_ref[...] = _dot(u_ref[...], w_ref[...])
    zc = acc_ref[HALO:HALO + tm, :]
    zp = acc_ref[HALO - 1:HALO - 1 + tm, :]
    zn = acc_ref[HALO + 1:HALO + 1 + tm, :]
    row = lax.broadcasted_iota(jnp.int32, (tm, 1), 0)
    pos = i % tiles_per_seq
    zp = jnp.where(jnp.logical_and(row == 0, pos == 0), 0.0, zp)
    zn = jnp.where(jnp.logical_and(row == tm - 1, pos == tiles_per_seq - 1), 0.0, zn)
    o_ref[...] = (zc + mu_ref[0:1, :] * (zp - zc) + mu_ref[1:2, :] * (zn - zc)).astype(o_ref.dtype)


def _plain_inproj_kernel(x_ref, mod_ref, w_ref, o_ref, u_ref, *, shift_row, scale_row):
    @pl.when(pl.program_id(1) == 0)
    def _():
        sc = 1.0 + mod_ref[scale_row:scale_row + 1, :]
        sh = mod_ref[shift_row:shift_row + 1, :]
        u_ref[...] = (x_ref[...] * sc + sh).astype(BF16)

    o_ref[...] = _dot(u_ref[...], w_ref[...]).astype(o_ref.dtype)


def _inproj_shift(x2, mod_l, w, mu, seq, out_dtype):
    t, d = x2.shape
    n = w.shape[1]
    tm = _pick_tile(seq, 1024, HALO)
    tn = _pick_tile(n, 512, LANES)
    tps = seq // tm
    hb = tm // HALO
    last_halo = t // HALO - 1
    kern = functools.partial(_inproj_kernel, tm=tm, tiles_per_seq=tps, shift_row=0, scale_row=1)
    return pl.pallas_call(
        kern,
        out_shape=jax.ShapeDtypeStruct((t, n), out_dtype),
        grid=(t // tm, n // tn),
        in_specs=[
            pl.BlockSpec((tm, d), lambda i, j: (i, 0)),
            pl.BlockSpec((HALO, d), lambda i, j: (jnp.maximum(i * hb - 1, 0), 0)),
            pl.BlockSpec((HALO, d), lambda i, j: (jnp.minimum((i + 1) * hb, last_halo), 0)),
            pl.BlockSpec((None, 6, d), lambda i, j: (i // tps, 0, 0)),
            pl.BlockSpec((d, tn), lambda i, j: (0, j)),
            pl.BlockSpec((2, tn), lambda i, j: (0, j)),
        ],
        out_specs=pl.BlockSpec((tm, tn), lambda i, j: (i, j)),
        scratch_shapes=[pltpu.VMEM((tm + 2 * HALO, d), BF16), pltpu.VMEM((tm + 2 * HALO, tn), F32)],
        compiler_params=_params("parallel", "arbitrary"),
        name="inproj_shift",
    )(x2, x2, x2, mod_l, w, mu)


def _inproj_plain(x2, mod_l, w, seq, out_dtype):
    t, d = x2.shape
    n = w.shape[1]
    tm = _pick_tile(seq, 1024, HALO)
    tn = _pick_tile(n, 512, LANES)
    tps = seq // tm
    kern = functools.partial(_plain_inproj_kernel, shift_row=0, scale_row=1)
    return pl.pallas_call(
        kern,
        out_shape=jax.ShapeDtypeStruct((t, n), out_dtype),
        grid=(t // tm, n // tn),
        in_specs=[
            pl.BlockSpec((tm, d), lambda i, j: (i, 0)),
            pl.BlockSpec((None, 6, d), lambda i, j: (i // tps, 0, 0)),
            pl.BlockSpec((d, tn), lambda i, j: (0, j)),
        ],
        out_specs=pl.BlockSpec((tm, tn), lambda i, j: (i, j)),
        scratch_shapes=[pltpu.VMEM((tm, d), BF16)],
        compiler_params=_params("parallel", "arbitrary"),
        name="inproj_plain",
    )(x2, mod_l, w)


def _wkv_constants():
    c, gc, gl = CHUNK, STACK_ROWS, GROUP_LANES
    assert gc == gl
    t = np.arange(c)
    lower = t[None, :] <= t[:, None]
    col_t = np.arange(gc) % c
    row_h = np.arange(gc) // c
    lane_h = np.arange(gl) // RWKV_HEAD
    tri = np.zeros((gc, gc), bool)
    tri[0:c, 0:c] = lower
    tri[c:2 * c, 0:c] = lower.T
    bd_lanes = lane_h[:, None] == lane_h[None, :]
    slab_h = np.stack([row_h[:, None] == lane_h[None, :], row_h[:, None] == row_h[None, :], bd_lanes, tri])
    slab_f = np.concatenate([
        col_t[None, :] < t[:, None], col_t[None, :] > t[:, None],
        col_t[None, :] <= t[:, None], col_t[None, :] >= t[:, None],
        col_t[None, :] == t[:, None], bd_lanes])
    return jnp.asarray(slab_h, BF16), jnp.asarray(slab_f, F32)


def _wkv_chain(d, gi, zrkv, th, xab, vrb, v0, ups, vecs, cbf, cf, q_ref, o_ref, bon_ref, has_vres):
    c, g, gc, gl = CHUNK, HEADS_PER_GROUP, STACK_ROWS, GROUP_LANES
    w = GROUPS_PER_STEP * gl
    sl = slice(gi * gl, (gi + 1) * gl)
    r = zrkv[:, gi * gl:(gi + 1) * gl]
    k = zrkv[:, w + gi * gl:w + (gi + 1) * gl]
    v = zrkv[:, 2 * w + gi * gl:2 * w + (gi + 1) * gl]
    w0, a0 = vecs[d:d + 1, sl], vecs[2 + d:3 + d, sl]
    k_k, k_a, r_k = vecs[4:5, sl], vecs[5:6, sl], vecs[6:7, sl]
    sm, bdm, bdl = cbf[0], cbf[1], cbf[2]
    tri = cbf[3, d * c:(d + 1) * c, 0:c]
    ms = cf[d * c:(d + 1) * c, :] > 0.5
    mi = cf[(2 + d) * c:(3 + d) * c, :] > 0.5
    eye_w = cf[4 * c:5 * c, :]

    wl = _dot(th, ups[d, :, sl])
    al = _dot(xab, ups[2 + d, :, sl])
    if has_vres:
        ml = _dot(vrb, ups[4, :, sl])
    yield
    if has_vres:
        v = v + (v0[:, sl] - v) * _sigmoid(vecs[7:8, sl] + ml)
    lw = -EXP_NEG_HALF * _sigmoid(w0 + wl)
    a = _sigmoid(a0 + al)
    kd = k * (1.0 + (a - 1.0) * k_a)
    kk0 = k * k_k
    sums = _dot(jnp.concatenate([*_split2(kk0 * kk0), *_split2(r * kd * r_k)], axis=0), bdl)
    cum3 = _dot(tri, jnp.concatenate(_split3(lw), axis=1))
    yield
    kk = kk0 * lax.rsqrt(jnp.maximum(sums[0:c] + sums[c:2 * c], 1e-24))
    bon_ref[:, sl] = (sums[2 * c:3 * c] + sums[3 * c:4 * c]) * v
    cum = cum3[:, 0:gl] + cum3[:, gl:2 * gl] + cum3[:, 2 * gl:3 * gl]
    tot = jnp.sum(lw, axis=0, keepdims=True)
    g_inv = jnp.exp(-cum)
    b = kk * a
    a_t = -(kk * jnp.exp(cum - lw))
    b_t = b * g_inv
    k_t = kd * g_inv
    r_t = r * jnp.exp(cum)
    g_rem = jnp.exp(tot - cum)
    bk_g = jnp.concatenate([b * g_rem, kd * g_rem], axis=0).astype(BF16)

    def stack(x):
        return jnp.concatenate([x.astype(BF16)] * g, axis=0) * sm

    def to_bd(xw):
        return jnp.concatenate([xw.astype(BF16)] * g, axis=0) * bdm

    ar = jnp.concatenate([a_t, r_t], axis=0).astype(BF16)
    nn = _dot_nt(ar, jnp.concatenate([stack(b_t), stack(k_t)], axis=0))
    q = q_ref[...]
    arq = _dot_nt(ar, q.astype(BF16))
    yield
    n_ab = jnp.where(ms, nn[0:c, 0:gc], 0.0)
    n_ak = jnp.where(ms, nn[0:c, gc:2 * gc], 0.0)
    n_rb = jnp.where(mi, nn[c:2 * c, 0:gc], 0.0).astype(BF16)
    n_rk = jnp.where(mi, nn[c:2 * c, gc:2 * gc], 0.0)
    pw = _dot(n_ab.astype(BF16), to_bd(n_ab))
    nv = _dot(jnp.concatenate([n_ak.astype(BF16), n_rk.astype(BF16)], axis=0), stack(v))
    yield
    inv = eye_w + n_ab
    steps = int(math.log2(c)) - 1
    for s in range(steps):
        rhs = to_bd(pw)
        if s < steps - 1:
            both = _dot(jnp.concatenate([pw.astype(BF16), inv.astype(BF16)], axis=0), rhs)
            yield
            pw = both[0:c]
            inv = inv + both[c:2 * c]
        else:
            last = _dot(inv.astype(BF16), rhs)
            yield
            inv = inv + last
    y = arq[0:c] + nv[0:c]
    u = _dot(inv.astype(BF16), stack(y))
    yield
    o_u = _dot(n_rb, stack(u))
    upd = _dot_tn(jnp.concatenate([u, v], axis=0).astype(BF16), bk_g)
    yield
    o_ref[:, sl] = arq[c:2 * c] + nv[c:2 * c] + o_u
    q_ref[...] = (q * jnp.exp(tot) + upd) * cf[5 * c:5 * c + gl, :]


def _wkv_kernel(*refs, has_vres):
    n_act = 4 if has_vres else 2
    acts = [list(refs[0:n_act]), list(refs[n_act:2 * n_act])]
    ups, vecs, cbf, cf = refs[2 * n_act:2 * n_act + 4]
    outs = refs[2 * n_act + 4:2 * n_act + 8]
    q_refs = refs[2 * n_act + 8]

    @pl.when(pl.program_id(2) == 0)
    def _():
        q_refs[...] = jnp.zeros_like(q_refs)

    chains = []
    for d in range(2):
        zrkv, xwa = acts[d][0:2]
        v0, vr = (acts[d][2], acts[d][3]) if has_vres else (None, None)
        th = jnp.tanh(xwa[:, 0:LANES]).astype(BF16)
        xab = xwa[:, LANES:2 * LANES].astype(BF16)
        vrb = vr[...].astype(BF16) if has_vres else None
        for gi in range(GROUPS_PER_STEP):
            chains.append(_wkv_chain(d, gi, zrkv, th, xab, vrb, v0, ups, vecs, cbf, cf,
                                     q_refs.at[d * GROUPS_PER_STEP + gi], outs[d], outs[2 + d], has_vres))
    live = chains
    while live:
        live = [ch for ch in live if next(ch, True) is None]


def _wkv(zs, zs_first, ups, vecs, batch, seq, d):
    t = zs.shape[0]
    has_vres = zs_first is not None
    c, gl = CHUNK, GROUP_LANES
    w = GROUPS_PER_STEP * gl
    assert seq % c == 0 and d % w == 0
    nc = seq // c
    ng = d // w
    lora_pair0 = 3 * d // (2 * LANES)
    vres_block = (3 * d + 4 * LANES + gl) // LANES

    def dir_specs(rev):
        rw = (lambda b, i: b * nc + (nc - 1 - i)) if rev else (lambda b, i: b * nc + i)
        specs = [
            pl.BlockSpec((c, 3 * w), lambda b, g, i: (rw(b, i), g)),
            pl.BlockSpec((c, 2 * LANES), lambda b, g, i: (rw(b, i), lora_pair0 + rev)),
        ]
        args = [zs, zs]
        if has_vres:
            specs += [
                pl.BlockSpec((c, w), lambda b, g, i: (rw(b, i), 3 * g + 2)),
                pl.BlockSpec((c, LANES), lambda b, g, i: (rw(b, i), vres_block)),
            ]
            args += [zs_first, zs]
        return specs, args

    fs, fa = dir_specs(0)
    bs, ba = dir_specs(1)
    cbf, cf = _wkv_constants()
    p_specs = [
        pl.BlockSpec((5, LANES, w), lambda b, g, i: (0, 0, g)),
        pl.BlockSpec((8, w), lambda b, g, i: (0, g)),
        pl.BlockSpec(cbf.shape, lambda b, g, i: (0, 0, 0)),
        pl.BlockSpec(cf.shape, lambda b, g, i: (0, 0)),
    ]
    out_f = pl.BlockSpec((c, w), lambda b, g, i: (b * nc + i, g))
    out_b = pl.BlockSpec((c, w), lambda b, g, i: (b * nc + (nc - 1 - i), g))
    shp = jax.ShapeDtypeStruct((t, d), F32)
    return pl.pallas_call(
        functools.partial(_wkv_kernel, has_vres=has_vres),
        out_shape=[shp, shp, shp, shp],
        grid=(batch, ng, nc),
        in_specs=fs + bs + p_specs,
        out_specs=[out_f, out_b, out_f, out_b],
        scratch_shapes=[pltpu.VMEM((2 * GROUPS_PER_STEP, gl, gl), F32)],
        compiler_params=_params("parallel", "parallel", "arbitrary"),
        name="wkv7_chunked",
    )(*fa, *ba, ups, vecs, cbf, cf)


def _rwkv_out_kernel(of_ref, ob_ref, bf_ref, bb_ref, xg_ref, gup_ref, lnw_ref, lnb_ref, bd_ref, o_ref):
    o = of_ref[...] + ob_ref[...]
    bd = bd_ref[...]
    inv_n = 1.0 / RWKV_HEAD
    mu = _dot(o, bd, HIGHEST) * inv_n
    oc = o - mu
    var = _dot(oc * oc, bd, HIGHEST) * inv_n
    o_n = oc * lax.rsqrt(var + GN_EPS) * lnw_ref[...] + lnb_ref[...]
    gate = _dot(_sigmoid(xg_ref[...]).astype(BF16), gup_ref[...])
    o_ref[...] = ((o_n + bf_ref[...] + bb_ref[...]) * gate).astype(o_ref.dtype)


def _rwkv_out(o_f, o_b, bon_f, bon_b, zs, gate_up, ln_w, ln_b, d):
    t = o_f.shape[0]
    gl = GROUP_LANES
    tm = _pick_tile(t, 1024, 8)
    xg_block = (3 * d + 4 * LANES) // gl
    bd = _wkv_constants()[1][5 * CHUNK:]
    big = pl.BlockSpec((tm, gl), lambda i, j: (i, j))
    return pl.pallas_call(
        _rwkv_out_kernel,
        out_shape=jax.ShapeDtypeStruct((t, d), BF16),
        grid=(t // tm, d // gl),
        in_specs=[
            big, big, big, big,
            pl.BlockSpec((tm, gl), lambda i, j: (i, xg_block)),
            pl.BlockSpec((gl, gl), lambda i, j: (0, j)),
            pl.BlockSpec((1, gl), lambda i, j: (0, j)),
            pl.BlockSpec((1, gl), lambda i, j: (0, j)),
            pl.BlockSpec((gl, gl), lambda i, j: (0, 0)),
        ],
        out_specs=big,
        compiler_params=_params("parallel", "parallel"),
        name="rwkv_out",
    )(o_f, o_b, bon_f, bon_b, zs, gate_up, ln_w, ln_b, bd)


def _attn_kernel(q_ref, k_ref, v_ref, cos_ref, sin_ref, lam_ref, sw_ref, o_ref, kr_ref, *, tq, lam_init):
    qi = pl.program_id(2)
    dh = DIFF_HEAD

    @pl.when(qi == 0)
    def _():
        cos = cos_ref[...]
        sin = sin_ref[...]
        for m in range(2):
            km = k_ref[:, m * dh:(m + 1) * dh].astype(F32)
            kr_ref[:, m * dh:(m + 1) * dh] = (km * cos + pltpu.roll(km, dh // 2, 1) * sin).astype(BF16)

    lam = (jnp.exp(jnp.sum(lam_ref[0:1, :] * lam_ref[1:2, :], axis=-1, keepdims=True))
           - jnp.exp(jnp.sum(lam_ref[2:3, :] * lam_ref[3:4, :], axis=-1, keepdims=True)) + lam_init)
    row0 = pl.multiple_of(qi * tq, tq)
    cq = cos_ref[pl.ds(row0, tq), :]
    sq = sin_ref[pl.ds(row0, tq), :]
    probs = []
    for m in range(2):
        qm = q_ref[:, m * dh:(m + 1) * dh].astype(F32)
        qm = (qm * cq + pltpu.roll(qm, dh // 2, 1) * sq) * (dh ** -0.5)
        s = _dot_nt(qm.astype(BF16), kr_ref[:, m * dh:(m + 1) * dh])
        p = jnp.exp(s - jnp.max(s, axis=-1, keepdims=True))
        probs.append((p, jnp.sum(p, axis=-1, keepdims=True)))
    (p1, l1), (p2, l2) = probs
    w = p1 * (1.0 / l1) - p2 * (lam / l2)
    o = _dot(w.astype(BF16), v_ref[...])
    o = o * lax.rsqrt(jnp.mean(o * o, axis=-1, keepdims=True) + SUBLN_EPS) * sw_ref[...]
    o_ref[...] = (o * (1.0 - lam_init)).astype(o_ref.dtype)


def _diff_attention(z_att, cos, sin_signed, lam_vecs, subln_w, batch, seq, d, lam_init):
    t = z_att.shape[0]
    hw = 2 * DIFF_HEAD
    nh = d // hw
    tq = _pick_tile(seq, 256, 8)
    nq = seq // tq
    kern = functools.partial(_attn_kernel, tq=tq, lam_init=lam_init)
    return pl.pallas_call(
        kern,
        out_shape=jax.ShapeDtypeStruct((t, d), BF16),
        grid=(batch, nh, nq),
        in_specs=[
            pl.BlockSpec((tq, hw), lambda b, h, i: (b * nq + i, h)),
            pl.BlockSpec((seq, hw), lambda b, h, i: (b, nh + h)),
            pl.BlockSpec((seq, hw), lambda b, h, i: (b, 2 * nh + h)),
            pl.BlockSpec((seq, DIFF_HEAD), lambda b, h, i: (0, 0)),
            pl.BlockSpec((seq, DIFF_HEAD), lambda b, h, i: (0, 0)),
            pl.BlockSpec((4, DIFF_HEAD), lambda b, h, i: (0, 0)),
            pl.BlockSpec((1, hw), lambda b, h, i: (0, 0)),
        ],
        out_specs=pl.BlockSpec((tq, hw), lambda b, h, i: (b * nq + i, h)),
        scratch_shapes=[pltpu.VMEM((seq, hw), BF16)],
        compiler_params=_params("parallel", "parallel", "arbitrary"),
        name="diff_attention",
    )(z_att, z_att, z_att, cos, sin_signed, lam_vecs, subln_w)


def _merge_kernel(ya_ref, yb_ref, pa_ref, pb_ref, ga_ref, gb_ref, o_ref):
    ma = _dot(ya_ref[...], pa_ref[...])
    mb = _dot(yb_ref[...], pb_ref[...])
    ga = _sigmoid(ga_ref[...].astype(F32))
    gb = _sigmoid(gb_ref[...].astype(F32))
    o_ref[...] = (ga * ma + gb * mb).astype(o_ref.dtype)


def _merge(y_a, y_b, proj_a, proj_b, z_att, d):
    t = y_a.shape[0]
    tm = _pick_tile(t, 1024, 8)
    tn = _pick_tile(d, 512, LANES)
    nb = d // tn
    return pl.pallas_call(
        _merge_kernel,
        out_shape=jax.ShapeDtypeStruct((t, d), BF16),
        grid=(t // tm, nb),
        in_specs=[
            pl.BlockSpec((tm, d), lambda i, j: (i, 0)),
            pl.BlockSpec((tm, d), lambda i, j: (i, 0)),
            pl.BlockSpec((d, tn), lambda i, j: (0, j)),
            pl.BlockSpec((d, tn), lambda i, j: (0, j)),
            pl.BlockSpec((tm, tn), lambda i, j: (i, 3 * nb + j)),
            pl.BlockSpec((tm, tn), lambda i, j: (i, 4 * nb + j)),
        ],
        out_specs=pl.BlockSpec((tm, tn), lambda i, j: (i, j)),
        compiler_params=_params("parallel", "arbitrary"),
        name="gated_merge",
    )(y_a, y_b, proj_a, proj_b, z_att, z_att)


def _mm_ln_kernel(a_ref, w_ref, x_ref, mod_ref, g_ref, b_ref, o_ref, acc_ref, *, nk, gate_row, alpha):
    kk = pl.program_id(1)

    @pl.when(kk == 0)
    def _():
        acc_ref[...] = jnp.zeros_like(acc_ref)

    acc_ref[...] += _dot(a_ref[...], w_ref[...])

    @pl.when(kk == nk - 1)
    def _():
        y = alpha * x_ref[...] + mod_ref[gate_row:gate_row + 1, :] * acc_ref[...]
        mu = jnp.mean(y, axis=-1, keepdims=True)
        yc = y - mu
        var = jnp.mean(yc * yc, axis=-1, keepdims=True)
        o_ref[...] = yc * lax.rsqrt(var + LN_EPS) * g_ref[...] + b_ref[...]


def _mm_residual_ln(a, w, x2, mod_l, gate_row, ln_g, ln_b, seq, alpha):
    t, kdim = a.shape
    d = w.shape[1]
    tm = _pick_tile(seq, 512, 8)
    tk = _pick_tile(kdim, 1536, LANES)
    nk = kdim // tk
    tps = seq // tm
    kern = functools.partial(_mm_ln_kernel, nk=nk, gate_row=gate_row, alpha=alpha)
    return pl.pallas_call(
        kern,
        out_shape=jax.ShapeDtypeStruct((t, d), F32),
        grid=(t // tm, nk),
        in_specs=[
            pl.BlockSpec((tm, tk), lambda i, k: (i, k)),
            pl.BlockSpec((tk, d), lambda i, k: (k, 0)),
            pl.BlockSpec((tm, d), lambda i, k: (i, 0)),
            pl.BlockSpec((None, 6, d), lambda i, k: (i // tps, 0, 0)),
            pl.BlockSpec((1, d), lambda i, k: (0, 0)),
            pl.BlockSpec((1, d), lambda i, k: (0, 0)),
        ],
        out_specs=pl.BlockSpec((tm, d), lambda i, k: (i, 0)),
        scratch_shapes=[pltpu.VMEM((tm, d), F32)],
        compiler_params=_params("parallel", "arbitrary"),
        name="matmul_residual_ln",
    )(a, w, x2, mod_l, ln_g, ln_b)


def _ffn_up_kernel(x_ref, mod_ref, wg_ref, wu_ref, o_ref, u_ref, *, shift_row, scale_row):
    @pl.when(pl.program_id(1) == 0)
    def _():
        sc = 1.0 + mod_ref[scale_row:scale_row + 1, :]
        sh = mod_ref[shift_row:shift_row + 1, :]
        u_ref[...] = (x_ref[...] * sc + sh).astype(BF16)

    u = u_ref[...]
    hg = _dot(u, wg_ref[...])
    hu = _dot(u, wu_ref[...])
    o_ref[...] = (hg * _sigmoid(hg) * hu).astype(o_ref.dtype)


def _ffn_up(x2, mod_l, w_gate, w_up, seq):
    t, d = x2.shape
    f = w_gate.shape[1]
    tm = _pick_tile(seq, 1024, 8)
    tn = _pick_tile(f, 512, LANES)
    tps = seq // tm
    kern = functools.partial(_ffn_up_kernel, shift_row=3, scale_row=4)
    return pl.pallas_call(
        kern,
        out_shape=jax.ShapeDtypeStruct((t, f), BF16),
        grid=(t // tm, f // tn),
        in_specs=[
            pl.BlockSpec((tm, d), lambda i, j: (i, 0)),
            pl.BlockSpec((None, 6, d), lambda i, j: (i // tps, 0, 0)),
            pl.BlockSpec((d, tn), lambda i, j: (0, j)),
            pl.BlockSpec((d, tn), lambda i, j: (0, j)),
        ],
        out_specs=pl.BlockSpec((tm, tn), lambda i, j: (i, j)),
        scratch_shapes=[pltpu.VMEM((tm, d), BF16)],
        compiler_params=_params("parallel", "arbitrary"),
        name="ffn_up",
    )(x2, mod_l, w_gate, w_up)


def _pad_cols(a, width):
    return jnp.pad(a, [(0, 0)] * (a.ndim - 1) + [(0, width - a.shape[-1])])


def _pad_rows(a, height):
    return jnp.pad(a, [(0, 0)] * (a.ndim - 2) + [(0, height - a.shape[-2]), (0, 0)])


def _rope_tables(seq):
    pos = jnp.arange(seq, dtype=F32)
    inv = ROPE_THETA ** (-jnp.arange(0, DIFF_HEAD, 2, dtype=F32) / DIFF_HEAD)
    ang = pos[:, None] * inv[None, :]
    emb = jnp.concatenate([ang, ang], axis=-1)
    sign = jnp.where(jnp.arange(DIFF_HEAD) < DIFF_HEAD // 2, -1.0, 1.0).astype(F32)
    return jnp.cos(emb), jnp.sin(emb) * sign[None, :]


def kernel(x, c, ada_w, ada_b, w_in, shift_mu_prev, shift_mu_next, decay_w0, decay_up, iclr_a0, iclr_up, gate_up, k_k, k_a, r_k, ln_x_w, ln_x_b, vres_down, vres_up, vres_v0, lambda_q1, lambda_k1, lambda_q2, lambda_k2, subln_w, proj_a, proj_b, w_out, ln1_g, ln1_b, ffn_w_gate, ffn_w_up, ffn_w_down, ln2_g, ln2_b):
    batch, seq, d = x.shape
    depth = ada_w.shape[0]
    t = batch * seq
    dl, il, gl_rank = decay_up.shape[2], iclr_up.shape[2], gate_up.shape[1]
    vl = vres_down.shape[2]
    assert max(dl, il, vl) <= LANES and gl_rank <= GROUP_LANES
    rwkv_cols = 3 * d + 2 * dl + 2 * il + gl_rank
    alpha = (2 * depth) ** 0.25

    mod = _modulation(c, ada_w, ada_b)
    cos, sin_signed = _rope_tables(seq)
    x2 = x.reshape(t, d)
    zs_first = None

    for l in range(depth):
        mod_l = mod[l]
        w_l = w_in[l]
        gw = GROUPS_PER_STEP * GROUP_LANES
        cuts = np.cumsum([3 * d, dl, dl, il, il, gl_rank])

        def regroup(a):
            rkv, xw_f, xw_b, xa_f, xa_b, xg = jnp.split(a[..., :rwkv_cols], cuts[:-1], axis=-1)
            lead = rkv.shape[:-1]
            rkv = jnp.swapaxes(rkv.reshape(*lead, 3, d // gw, gw), -3, -2).reshape(*lead, 3 * d)
            return [rkv, _pad_cols(xw_f, LANES), _pad_cols(xa_f, LANES), _pad_cols(xw_b, LANES),
                    _pad_cols(xa_b, LANES), _pad_cols(xg, GROUP_LANES)]

        w_parts = regroup(w_l)
        if l > 0:
            w_parts.append(_pad_cols(vres_down[l - 1], LANES))
        else:
            w_parts.append(jnp.zeros((d, LANES), F32))
        n_rwkv = sum(p.shape[1] for p in w_parts)
        n_pad = -(-n_rwkv // 512) * 512
        w_rwkv = _pad_cols(jnp.concatenate(w_parts, axis=1), n_pad).astype(BF16)
        mu = jnp.stack([
            _pad_cols(jnp.concatenate(regroup(shift_mu_prev[l]), axis=0), n_pad),
            _pad_cols(jnp.concatenate(regroup(shift_mu_next[l]), axis=0), n_pad)])
        w_att = w_l[:, rwkv_cols:].astype(BF16)

        zs = _inproj_shift(x2, mod_l, w_rwkv, mu, seq, F32)
        z_att = _inproj_plain(x2, mod_l, w_att, seq, BF16)

        vup = _pad_rows(vres_up[l - 1], LANES) if l > 0 else jnp.zeros((LANES, d), F32)
        vv0 = vres_v0[l - 1] if l > 0 else jnp.zeros((d,), F32)
        ups = jnp.concatenate([_pad_rows(decay_up[l], LANES), _pad_rows(iclr_up[l], LANES), vup[None]],
                              axis=0).astype(BF16)
        vecs = jnp.stack([decay_w0[l, 0], decay_w0[l, 1], iclr_a0[l, 0], iclr_a0[l, 1],
                          k_k[l], k_a[l], r_k[l].reshape(d), vv0])
        o_f, o_b, bon_f, bon_b = _wkv(zs, zs_first if l > 0 else None, ups, vecs, batch, seq, d)
        if l == 0:
            zs_first = zs
        y_a = _rwkv_out(o_f, o_b, bon_f, bon_b, zs, _pad_rows(gate_up[l], GROUP_LANES).astype(BF16),
                        ln_x_w[l].reshape(1, d), ln_x_b[l].reshape(1, d), d)

        lam_init = 0.8 - 0.6 * math.exp(-0.3 * l)
        lam_vecs = jnp.stack([lambda_q1[l], lambda_k1[l], lambda_q2[l], lambda_k2[l]])
        y_b = _diff_attention(z_att, cos, sin_signed, lam_vecs, subln_w[l].reshape(1, -1),
                              batch, seq, d, lam_init)

        merged = _merge(y_a, y_b, proj_a[l].astype(BF16), proj_b[l].astype(BF16), z_att, d)
        x2 = _mm_residual_ln(merged, w_out[l].astype(BF16), x2, mod_l, 2,
                             ln1_g[l].reshape(1, d), ln1_b[l].reshape(1, d), seq, alpha)

        h = _ffn_up(x2, mod_l, ffn_w_gate[l].astype(BF16), ffn_w_up[l].astype(BF16), seq)
        x2 = _mm_residual_ln(h, ffn_w_down[l].astype(BF16), x2, mod_l, 5,
                             ln2_g[l].reshape(1, d), ln2_b[l].reshape(1, d), seq, alpha)

    return x2.reshape(batch, seq, d)
```

```python
import functools
import math

import numpy as np
import jax
import jax.numpy as jnp
from jax import lax
from jax.experimental import pallas as pl
from jax.experimental.pallas import tpu as pltpu

F32 = jnp.float32
BF16 = jnp.bfloat16
HIGHEST = lax.Precision.HIGHEST

RWKV_HEAD = 64
DIFF_HEAD = 128
ROPE_THETA = 10000.0
SUBLN_EPS = 1e-5
LN_EPS = 1e-5
GN_EPS = 1e-5 * RWKV_HEAD
EXP_NEG_HALF = math.exp(-0.5)
LOG2E = math.log2(math.e)

LANES = 128
MXU_DIM = 256
VMEM_LIMIT = 56 * 1024 * 1024

CHUNK = 64
HEADS_PER_GROUP = MXU_DIM // RWKV_HEAD
GROUP_LANES = HEADS_PER_GROUP * RWKV_HEAD
STACK_ROWS = HEADS_PER_GROUP * CHUNK
MAX_GROUPS_PER_STEP = 4
HALO = 16


def _sigmoid(x):
    return 1.0 / (1.0 + jnp.exp(-x))


def _dot(a, b, precision=None):
    return jnp.dot(a, b, preferred_element_type=F32, precision=precision)


def _dot_nt(a, b):
    return lax.dot_general(a, b, (((1,), (1,)), ((), ())), preferred_element_type=F32)


def _dot_tn(a, b):
    return lax.dot_general(a, b, (((0,), (0,)), ((), ())), preferred_element_type=F32)


def _split2(x):
    hi = x.astype(BF16)
    return hi, (x - hi.astype(F32)).astype(BF16)


def _split3(x):
    hi = x.astype(BF16)
    rest = x - hi.astype(F32)
    mid = rest.astype(BF16)
    return hi, mid, (rest - mid.astype(F32)).astype(BF16)


def _params(*semantics):
    return pltpu.CompilerParams(dimension_semantics=semantics, vmem_limit_bytes=VMEM_LIMIT)


def _pick_tile(n, target, quantum):
    best = None
    t = quantum
    while t <= min(n, target):
        if n % t == 0:
            best = t
        t += quantum
    assert best is not None, (n, target, quantum)
    return best


def _mod_kernel(c_ref, w_ref, b_ref, o_ref):
    c = c_ref[...]
    c_act = c * _sigmoid(c)
    o_ref[...] = _dot(c_act, w_ref[...], HIGHEST) + b_ref[...]


def _modulation(c, ada_w, ada_b):
    nl, d, n6 = ada_w.shape
    b = c.shape[0]
    rows = -(-b // 8) * 8
    c_pad = jnp.pad(c, ((0, rows - b), (0, 0)))
    tn = _pick_tile(n6, 512, LANES)
    out = pl.pallas_call(
        _mod_kernel,
        out_shape=jax.ShapeDtypeStruct((nl, rows, n6), F32),
        grid=(nl, n6 // tn),
        in_specs=[
            pl.BlockSpec((rows, d), lambda l, j: (0, 0)),
            pl.BlockSpec((None, d, tn), lambda l, j: (l, 0, j)),
            pl.BlockSpec((None, 1, tn), lambda l, j: (l, 0, j)),
        ],
        out_specs=pl.BlockSpec((None, rows, tn), lambda l, j: (l, 0, j)),
        compiler_params=_params("parallel", "parallel"),
        name="adaln_mod",
    )(c_pad, ada_w, ada_b.reshape(nl, 1, n6))
    return out[:, :b].reshape(nl, b, 6, d)


def _inproj_kernel(x_ref, xp_ref, xn_ref, mod_ref, w_ref, mu_ref, o_ref, u_ref, acc_ref, *,
                   tm, tiles_per_seq, shift_row, scale_row):
    i = pl.program_id(0)
    j = pl.program_id(1)

    @pl.when(j == 0)
    def _():
        sc = 1.0 + mod_ref[scale_row:scale_row + 1, :]
        sh = mod_ref[shift_row:shift_row + 1, :]
        u_ref[0:HALO, :] = (xp_ref[...] * sc + sh).astype(BF16)
        u_ref[HALO:HALO + tm, :] = (x_ref[...] * sc + sh).astype(BF16)
        u_ref[HALO + tm:, :] = (xn_ref[...] * sc + sh).astype(BF16)

    acc_ref[...] = _dot(u_ref[...], w_ref[...])
    zc = acc_ref[HALO:HALO + tm, :]
    zp = acc_ref[HALO - 1:HALO - 1 + tm, :]
    zn = acc_ref[HALO + 1:HALO + 1 + tm, :]
    row = lax.broadcasted_iota(jnp.int32, (tm, 1), 0)
    pos = i % tiles_per_seq
    zp = jnp.where(jnp.logical_and(row == 0, pos == 0), 0.0, zp)
    zn = jnp.where(jnp.logical_and(row == tm - 1, pos == tiles_per_seq - 1), 0.0, zn)
    o_ref[...] = (zc + mu_ref[0:1, :] * (zp - zc) + mu_ref[1:2, :] * (zn - zc)).astype(o_ref.dtype)


def _plain_inproj_kernel(x_ref, mod_ref, w_ref, o_ref, u_ref, *, shift_row, scale_row):
    @pl.when(pl.program_id(1) == 0)
    def _():
        sc = 1.0 + mod_ref[scale_row:scale_row + 1, :]
        sh = mod_ref[shift_row:shift_row + 1, :]
        u_ref[...] = (x_ref[...] * sc + sh).astype(BF16)

    o_ref[...] = _dot(u_ref[...], w_ref[...]).astype(o_ref.dtype)


def _inproj_shift(x2, mod_l, w, mu, seq, out_dtype):
    t, d = x2.shape
    n = w.shape[1]
    tm = _pick_tile(seq, 1024, HALO)
    tn = _pick_tile(n, 512, LANES)
    tps = seq // tm
    hb = tm // HALO
    last_halo = t // HALO - 1
    kern = functools.partial(_inproj_kernel, tm=tm, tiles_per_seq=tps, shift_row=0, scale_row=1)
    return pl.pallas_call(
        kern,
        out_shape=jax.ShapeDtypeStruct((t, n), out_dtype),
        grid=(t // tm, n // tn),
        in_specs=[
            pl.BlockSpec((tm, d), lambda i, j: (i, 0)),
            pl.BlockSpec((HALO, d), lambda i, j: (jnp.maximum(i * hb - 1, 0), 0)),
            pl.BlockSpec((HALO, d), lambda i, j: (jnp.minimum((i + 1) * hb, last_halo), 0)),
            pl.BlockSpec((None, 6, d), lambda i, j: (i // tps, 0, 0)),
            pl.BlockSpec((d, tn), lambda i, j: (0, j)),
            pl.BlockSpec((2, tn), lambda i, j: (0, j)),
        ],
        out_specs=pl.BlockSpec((tm, tn), lambda i, j: (i, j)),
        scratch_shapes=[pltpu.VMEM((tm + 2 * HALO, d), BF16), pltpu.VMEM((tm + 2 * HALO, tn), F32)],
        compiler_params=_params("parallel", "arbitrary"),
        name="inproj_shift",
    )(x2, x2, x2, mod_l, w, mu)


def _inproj_plain(x2, mod_l, w, seq, out_dtype):
    t, d = x2.shape
    n = w.shape[1]
    tm = _pick_tile(seq, 1024, HALO)
    tn = _pick_tile(n, 512, LANES)
    tps = seq // tm
    kern = functools.partial(_plain_inproj_kernel, shift_row=0, scale_row=1)
    return pl.pallas_call(
        kern,
        out_shape=jax.ShapeDtypeStruct((t, n), out_dtype),
        grid=(t // tm, n // tn),
        in_specs=[
            pl.BlockSpec((tm, d), lambda i, j: (i, 0)),
            pl.BlockSpec((None, 6, d), lambda i, j: (i // tps, 0, 0)),
            pl.BlockSpec((d, tn), lambda i, j: (0, j)),
        ],
        out_specs=pl.BlockSpec((tm, tn), lambda i, j: (i, j)),
        scratch_shapes=[pltpu.VMEM((tm, d), BF16)],
        compiler_params=_params("parallel", "arbitrary"),
        name="inproj_plain",
    )(x2, mod_l, w)


def _wkv_constants():
    c, gc, gl = CHUNK, STACK_ROWS, GROUP_LANES
    assert gc == gl
    t = np.arange(c)
    lower = t[None, :] <= t[:, None]
    col_t = np.arange(gc) % c
    row_h = np.arange(gc) // c
    lane_h = np.arange(gl) // RWKV_HEAD
    tri = np.zeros((gc, gc), bool)
    tri[0:c, 0:c] = lower
    tri[c:2 * c, 0:c] = lower.T
    bd_lanes = lane_h[:, None] == lane_h[None, :]
    slab_h = np.stack([row_h[:, None] == lane_h[None, :], row_h[:, None] == row_h[None, :], bd_lanes, tri])
    slab_f = np.concatenate([
        col_t[None, :] < t[:, None], col_t[None, :] > t[:, None],
        col_t[None, :] <= t[:, None], col_t[None, :] >= t[:, None],
        col_t[None, :] == t[:, None], bd_lanes])
    return jnp.asarray(slab_h, BF16), jnp.asarray(slab_f, F32)


def _groups_per_step(d):
    groups = d // GROUP_LANES
    return max(g for g in range(1, MAX_GROUPS_PER_STEP + 1) if groups % g == 0)


def _wkv_chain(d, gi, gps, zrkv, th, xab, vrb, v0, ups, vecs, cbf, cf, q_ref, o_ref, bon_ref, has_vres):
    c, g, gc, gl = CHUNK, HEADS_PER_GROUP, STACK_ROWS, GROUP_LANES
    w = gps * gl
    sl = slice(gi * gl, (gi + 1) * gl)
    r = zrkv[:, gi * gl:(gi + 1) * gl]
    k = zrkv[:, w + gi * gl:w + (gi + 1) * gl]
    v = zrkv[:, 2 * w + gi * gl:2 * w + (gi + 1) * gl]
    w0, a0 = vecs[d:d + 1, sl], vecs[2 + d:3 + d, sl]
    k_k, k_a, r_k = vecs[4:5, sl], vecs[5:6, sl], vecs[6:7, sl]
    sm, bdm, bdl = cbf[0], cbf[1], cbf[2]
    tri = cbf[3, d * c:(d + 1) * c, 0:c]
    ms = cf[d * c:(d + 1) * c, :] > 0.5
    mi = cf[(2 + d) * c:(3 + d) * c, :] > 0.5
    eye_w = cf[4 * c:5 * c, :]

    wl = _dot(th, ups[d, :, sl])
    al = _dot(xab, ups[2 + d, :, sl])
    if has_vres:
        ml = _dot(vrb, ups[4, :, sl])
    yield
    if has_vres:
        v = v + (v0[:, sl] - v) * _sigmoid(vecs[7:8, sl] + ml)
    lw = -EXP_NEG_HALF * _sigmoid(w0 + wl)
    a = _sigmoid(a0 + al)
    kd = k * (1.0 + (a - 1.0) * k_a)
    kk0 = k * k_k
    sums = _dot(jnp.concatenate([*_split2(kk0 * kk0), *_split2(r * kd * r_k)], axis=0), bdl)
    cum3 = _dot(tri, jnp.concatenate(_split3(lw), axis=1))
    yield
    kk = kk0 * lax.rsqrt(jnp.maximum(sums[0:c] + sums[c:2 * c], 1e-24))
    bon_ref[:, sl] = (sums[2 * c:3 * c] + sums[3 * c:4 * c]) * v
    cum = cum3[:, 0:gl] + cum3[:, gl:2 * gl] + cum3[:, 2 * gl:3 * gl]
    tot = jnp.sum(lw, axis=0, keepdims=True)
    g_inv = jnp.exp(-cum)
    b = kk * a
    a_t = -(kk * jnp.exp(cum - lw))
    b_t = b * g_inv
    k_t = kd * g_inv
    r_t = r * jnp.exp(cum)
    g_rem = jnp.exp(tot - cum)
    bk_g = jnp.concatenate([b * g_rem, kd * g_rem], axis=0).astype(BF16)

    def stack(x):
        return jnp.concatenate([x.astype(BF16)] * g, axis=0) * sm

    def to_bd(xw):
        return jnp.concatenate([xw.astype(BF16)] * g, axis=0) * bdm

    ar = jnp.concatenate([a_t, r_t], axis=0).astype(BF16)
    nn = _dot_nt(ar, jnp.concatenate([stack(b_t), stack(k_t)], axis=0))
    q = q_ref[...]
    arq = _dot_nt(ar, q.astype(BF16))
    yield
    n_ab = jnp.where(ms, nn[0:c, 0:gc], 0.0)
    n_ak = jnp.where(ms, nn[0:c, gc:2 * gc], 0.0)
    n_rb = jnp.where(mi, nn[c:2 * c, 0:gc], 0.0).astype(BF16)
    n_rk = jnp.where(mi, nn[c:2 * c, gc:2 * gc], 0.0)
    pw = _dot(n_ab.astype(BF16), to_bd(n_ab))
    nv = _dot(jnp.concatenate([n_ak.astype(BF16), n_rk.astype(BF16)], axis=0), stack(v))
    yield
    inv = eye_w + n_ab
    steps = int(math.log2(c)) - 1
    for s in range(steps):
        rhs = to_bd(pw)
        if s < steps - 1:
            both = _dot(jnp.concatenate([pw.astype(BF16), inv.astype(BF16)], axis=0), rhs)
            yield
            pw = both[0:c]
            inv = inv + both[c:2 * c]
        else:
            last = _dot(inv.astype(BF16), rhs)
            yield
            inv = inv + last
    y = arq[0:c] + nv[0:c]
    u = _dot(inv.astype(BF16), stack(y))
    yield
    o_u = _dot(n_rb, stack(u))
    upd = _dot_tn(jnp.concatenate([u, v], axis=0).astype(BF16), bk_g)
    yield
    o_ref[:, sl] = arq[c:2 * c] + nv[c:2 * c] + o_u
    q_ref[...] = (q * jnp.exp(tot) + upd) * cf[5 * c:5 * c + gl, :]


def _wkv_kernel(*refs, has_vres, gps):
    n_act = 4 if has_vres else 2
    acts = [list(refs[0:n_act]), list(refs[n_act:2 * n_act])]
    ups, vecs, cbf, cf = refs[2 * n_act:2 * n_act + 4]
    outs = refs[2 * n_act + 4:2 * n_act + 8]
    q_refs = refs[2 * n_act + 8]

    @pl.when(pl.program_id(2) == 0)
    def _():
        q_refs[...] = jnp.zeros_like(q_refs)

    chains = []
    for d in range(2):
        zrkv, xwa = acts[d][0:2]
        v0, vr = (acts[d][2], acts[d][3]) if has_vres else (None, None)
        th = jnp.tanh(xwa[:, 0:LANES]).astype(BF16)
        xab = xwa[:, LANES:2 * LANES].astype(BF16)
        vrb = vr[...].astype(BF16) if has_vres else None
        for gi in range(gps):
            chains.append(_wkv_chain(d, gi, gps, zrkv, th, xab, vrb, v0, ups, vecs, cbf, cf,
                                     q_refs.at[d * gps + gi], outs[d], outs[2 + d], has_vres))
    live = chains
    while live:
        live = [ch for ch in live if next(ch, True) is None]


def _wkv(zs, zs_first, ups, vecs, batch, seq, d):
    t = zs.shape[0]
    has_vres = zs_first is not None
    c, gl = CHUNK, GROUP_LANES
    gps = _groups_per_step(d)
    w = gps * gl
    assert seq % c == 0 and d % w == 0
    nc = seq // c
    ng = d // w
    lora_pair0 = 3 * d // (2 * LANES)
    vres_block = (3 * d + 4 * LANES + gl) // LANES

    def dir_specs(rev):
        rw = (lambda b, i: b * nc + (nc - 1 - i)) if rev else (lambda b, i: b * nc + i)
        specs = [
            pl.BlockSpec((c, 3 * w), lambda b, g, i: (rw(b, i), g)),
            pl.BlockSpec((c, 2 * LANES), lambda b, g, i: (rw(b, i), lora_pair0 + rev)),
        ]
        args = [zs, zs]
        if has_vres:
            specs += [
                pl.BlockSpec((c, w), lambda b, g, i: (rw(b, i), 3 * g + 2)),
                pl.BlockSpec((c, LANES), lambda b, g, i: (rw(b, i), vres_block)),
            ]
            args += [zs_first, zs]
        return specs, args

    fs, fa = dir_specs(0)
    bs, ba = dir_specs(1)
    cbf, cf = _wkv_constants()
    p_specs = [
        pl.BlockSpec((5, LANES, w), lambda b, g, i: (0, 0, g)),
        pl.BlockSpec((8, w), lambda b, g, i: (0, g)),
        pl.BlockSpec(cbf.shape, lambda b, g, i: (0, 0, 0)),
        pl.BlockSpec(cf.shape, lambda b, g, i: (0, 0)),
    ]
    out_f = pl.BlockSpec((c, w), lambda b, g, i: (b * nc + i, g))
    out_b = pl.BlockSpec((c, w), lambda b, g, i: (b * nc + (nc - 1 - i), g))
    shp = jax.ShapeDtypeStruct((t, d), F32)
    return pl.pallas_call(
        functools.partial(_wkv_kernel, has_vres=has_vres, gps=gps),
        out_shape=[shp, shp, shp, shp],
        grid=(batch, ng, nc),
        in_specs=fs + bs + p_specs,
        out_specs=[out_f, out_b, out_f, out_b],
        scratch_shapes=[pltpu.VMEM((2 * gps, gl, gl), F32)],
        compiler_params=_params("parallel", "parallel", "arbitrary"),
        name="wkv7_chunked",
    )(*fa, *ba, ups, vecs, cbf, cf)


def _rwkv_out_kernel(of_ref, ob_ref, bf_ref, bb_ref, xg_ref, gup_ref, lnw_ref, lnb_ref, bd_ref, o_ref):
    o = of_ref[...] + ob_ref[...]
    bd = bd_ref[...]
    inv_n = 1.0 / RWKV_HEAD
    tm = o.shape[0]

    def head_sum(x):
        s2 = _dot(jnp.concatenate(_split2(x), axis=0), bd)
        return s2[0:tm] + s2[tm:2 * tm]

    mu = head_sum(o) * inv_n
    oc = o - mu
    var = head_sum(oc * oc) * inv_n
    o_n = oc * lax.rsqrt(var + GN_EPS) * lnw_ref[...] + lnb_ref[...]
    gate = _dot(_sigmoid(xg_ref[...]).astype(BF16), gup_ref[...])
    o_ref[...] = ((o_n + bf_ref[...] + bb_ref[...]) * gate).astype(o_ref.dtype)


def _rwkv_out(o_f, o_b, bon_f, bon_b, zs, gate_up, ln_w, ln_b, d):
    t = o_f.shape[0]
    gl = GROUP_LANES
    tm = _pick_tile(t, 1024, 8)
    xg_block = (3 * d + 4 * LANES) // gl
    bd = _wkv_constants()[0][2]
    big = pl.BlockSpec((tm, gl), lambda i, j: (i, j))
    return pl.pallas_call(
        _rwkv_out_kernel,
        out_shape=jax.ShapeDtypeStruct((t, d), BF16),
        grid=(t // tm, d // gl),
        in_specs=[
            big, big, big, big,
            pl.BlockSpec((tm, gl), lambda i, j: (i, xg_block)),
            pl.BlockSpec((gl, gl), lambda i, j: (0, j)),
            pl.BlockSpec((1, gl), lambda i, j: (0, j)),
            pl.BlockSpec((1, gl), lambda i, j: (0, j)),
            pl.BlockSpec((gl, gl), lambda i, j: (0, 0)),
        ],
        out_specs=big,
        compiler_params=_params("parallel", "parallel"),
        name="rwkv_out",
    )(o_f, o_b, bon_f, bon_b, zs, gate_up, ln_w, ln_b, bd)


def _attn_kernel(q_ref, k_ref, v_ref, cos_ref, sin_ref, lam_ref, sw_ref, o_ref, kr_ref, *, tq, tk, lam_init):
    qi = pl.program_id(2)
    dh = DIFF_HEAD

    @pl.when(qi == 0)
    def _():
        blk = 512 if k_ref.shape[0] % 512 == 0 else k_ref.shape[0]
        for c0 in range(0, k_ref.shape[0], blk):
            cos = cos_ref[c0:c0 + blk, :]
            sin = sin_ref[c0:c0 + blk, :]
            for m in range(2):
                km = k_ref[c0:c0 + blk, m * dh:(m + 1) * dh].astype(F32)
                kr = km * cos + pltpu.roll(km, dh // 2, 1) * sin
                kr_ref[m, :, c0:c0 + blk] = kr.T.astype(BF16)

    lam = (jnp.exp(jnp.sum(lam_ref[0:1, :] * lam_ref[1:2, :], axis=-1, keepdims=True))
           - jnp.exp(jnp.sum(lam_ref[2:3, :] * lam_ref[3:4, :], axis=-1, keepdims=True)) + lam_init)
    row0 = pl.multiple_of(qi * tq, tq)
    cq = cos_ref[pl.ds(row0, tq), :]
    sq = sin_ref[pl.ds(row0, tq), :]
    qs = []
    for m in range(2):
        qm = q_ref[:, m * dh:(m + 1) * dh].astype(F32)
        qm = (qm * cq + pltpu.roll(qm, dh // 2, 1) * sq) * (dh ** -0.5 * LOG2E)
        qs.append(qm.astype(BF16))
    nkv = k_ref.shape[0] // tk

    def lane_fold(x, op):
        out = x[:, 0:LANES]
        for cidx in range(1, tk // LANES):
            out = op(out, x[:, cidx * LANES:(cidx + 1) * LANES])
        return out

    mx, lp, acc = [None, None], [None, None], [None, None]
    for j in range(nkv):
        vt = v_ref[j * tk:(j + 1) * tk, :]
        for m in range(2):
            s = _dot(qs[m], kr_ref[m, :, j * tk:(j + 1) * tk])
            tile_max = jnp.max(lane_fold(s, jnp.maximum), axis=-1, keepdims=True)
            m_new = tile_max if j == 0 else jnp.maximum(mx[m], tile_max)
            p = jnp.exp2(s - m_new)
            psum = lane_fold(p, jnp.add)
            pv = _dot(p.astype(BF16), vt)
            if j == 0:
                lp[m], acc[m] = psum, pv
            else:
                alpha = jnp.exp2(mx[m] - m_new)
                lp[m] = alpha * lp[m] + psum
                acc[m] = alpha * acc[m] + pv
            mx[m] = m_new
    l1 = jnp.sum(lp[0], axis=-1, keepdims=True)
    l2 = jnp.sum(lp[1], axis=-1, keepdims=True)
    o = acc[0] * (1.0 / l1) - acc[1] * (lam / l2)
    o = o * lax.rsqrt(jnp.mean(o * o, axis=-1, keepdims=True) + SUBLN_EPS) * sw_ref[...]
    o_ref[...] = (o * (1.0 - lam_init)).astype(o_ref.dtype)


def _diff_attention(z_att, cos, sin_signed, lam_vecs, subln_w, batch, seq, d, lam_init):
    t = z_att.shape[0]
    hw = 2 * DIFF_HEAD
    nh = d // hw
    tq = _pick_tile(seq, 256, 8)
    nq = seq // tq
    tk = _pick_tile(seq, MXU_DIM, LANES)
    kern = functools.partial(_attn_kernel, tq=tq, tk=tk, lam_init=lam_init)
    return pl.pallas_call(
        kern,
        out_shape=jax.ShapeDtypeStruct((t, d), BF16),
        grid=(batch, nh, nq),
        in_specs=[
            pl.BlockSpec((tq, hw), lambda b, h, i: (b * nq + i, h)),
            pl.BlockSpec((seq, hw), lambda b, h, i: (b, nh + h)),
            pl.BlockSpec((seq, hw), lambda b, h, i: (b, 2 * nh + h)),
            pl.BlockSpec((seq, DIFF_HEAD), lambda b, h, i: (0, 0)),
            pl.BlockSpec((seq, DIFF_HEAD), lambda b, h, i: (0, 0)),
            pl.BlockSpec((4, DIFF_HEAD), lambda b, h, i: (0, 0)),
            pl.BlockSpec((1, hw), lambda b, h, i: (0, 0)),
        ],
        out_specs=pl.BlockSpec((tq, hw), lambda b, h, i: (b * nq + i, h)),
        scratch_shapes=[pltpu.VMEM((2, DIFF_HEAD, seq), BF16)],
        compiler_params=_params("parallel", "parallel", "arbitrary"),
        name="diff_attention",
    )(z_att, z_att, z_att, cos, sin_signed, lam_vecs, subln_w)


def _merge_kernel(ya_ref, yb_ref, pa_ref, pb_ref, ga_ref, gb_ref, o_ref):
    ma = _dot(ya_ref[...], pa_ref[...])
    mb = _dot(yb_ref[...], pb_ref[...])
    ga = _sigmoid(ga_ref[...].astype(F32))
    gb = _sigmoid(gb_ref[...].astype(F32))
    o_ref[...] = (ga * ma + gb * mb).astype(o_ref.dtype)


def _merge(y_a, y_b, proj_a, proj_b, z_att, d):
    t = y_a.shape[0]
    tm = _pick_tile(t, 1024, 8)
    tn = _pick_tile(d, 512, LANES)
    nb = d // tn
    return pl.pallas_call(
        _merge_kernel,
        out_shape=jax.ShapeDtypeStruct((t, d), BF16),
        grid=(t // tm, nb),
        in_specs=[
            pl.BlockSpec((tm, d), lambda i, j: (i, 0)),
            pl.BlockSpec((tm, d), lambda i, j: (i, 0)),
            pl.BlockSpec((d, tn), lambda i, j: (0, j)),
            pl.BlockSpec((d, tn), lambda i, j: (0, j)),
            pl.BlockSpec((tm, tn), lambda i, j: (i, 3 * nb + j)),
            pl.BlockSpec((tm, tn), lambda i, j: (i, 4 * nb + j)),
        ],
        out_specs=pl.BlockSpec((tm, tn), lambda i, j: (i, j)),
        compiler_params=_params("parallel", "arbitrary"),
        name="gated_merge",
    )(y_a, y_b, proj_a, proj_b, z_att, z_att)


def _mm_ln_kernel(a_ref, w_ref, x_ref, mod_ref, g_ref, b_ref, o_ref, acc_ref, *, nk, gate_row, alpha):
    kk = pl.program_id(1)

    @pl.when(kk == 0)
    def _():
        acc_ref[...] = jnp.zeros_like(acc_ref)

    acc_ref[...] += _dot(a_ref[...], w_ref[...])

    @pl.when(kk == nk - 1)
    def _():
        y = alpha * x_ref[...] + mod_ref[gate_row:gate_row + 1, :] * acc_ref[...]
        mu = jnp.mean(y, axis=-1, keepdims=True)
        yc = y - mu
        var = jnp.mean(yc * yc, axis=-1, keepdims=True)
        o_ref[...] = yc * lax.rsqrt(var + LN_EPS) * g_ref[...] + b_ref[...]


def _mm_residual_ln(a, w, x2, mod_l, gate_row, ln_g, ln_b, seq, alpha):
    t, kdim = a.shape
    d = w.shape[1]
    tm = _pick_tile(seq, 512, 8)
    tk = _pick_tile(kdim, 1536, LANES)
    nk = kdim // tk
    tps = seq // tm
    kern = functools.partial(_mm_ln_kernel, nk=nk, gate_row=gate_row, alpha=alpha)
    return pl.pallas_call(
        kern,
        out_shape=jax.ShapeDtypeStruct((t, d), F32),
        grid=(t // tm, nk),
        in_specs=[
            pl.BlockSpec((tm, tk), lambda i, k: (i, k)),
            pl.BlockSpec((tk, d), lambda i, k: (k, 0)),
            pl.BlockSpec((tm, d), lambda i, k: (i, 0)),
            pl.BlockSpec((None, 6, d), lambda i, k: (i // tps, 0, 0)),
            pl.BlockSpec((1, d), lambda i, k: (0, 0)),
            pl.BlockSpec((1, d), lambda i, k: (0, 0)),
        ],
        out_specs=pl.BlockSpec((tm, d), lambda i, k: (i, 0)),
        scratch_shapes=[pltpu.VMEM((tm, d), F32)],
        compiler_params=_params("parallel", "arbitrary"),
        name="matmul_residual_ln",
    )(a, w, x2, mod_l, ln_g, ln_b)


def _ffn_up_kernel(x_ref, mod_ref, wg_ref, wu_ref, o_ref, u_ref, *, shift_row, scale_row):
    @pl.when(pl.program_id(1) == 0)
    def _():
        sc = 1.0 + mod_ref[scale_row:scale_row + 1, :]
        sh = mod_ref[shift_row:shift_row + 1, :]
        u_ref[...] = (x_ref[...] * sc + sh).astype(BF16)

    u = u_ref[...]
    hg = _dot(u, wg_ref[...])
    hu = _dot(u, wu_ref[...])
    o_ref[...] = (hg * _sigmoid(hg) * hu).astype(o_ref.dtype)


def _ffn_up(x2, mod_l, w_gate, w_up, seq):
    t, d = x2.shape
    f = w_gate.shape[1]
    tm = _pick_tile(seq, 1024, 8)
    tn = _pick_tile(f, 512, LANES)
    tps = seq // tm
    kern = functools.partial(_ffn_up_kernel, shift_row=3, scale_row=4)
    return pl.pallas_call(
        kern,
        out_shape=jax.ShapeDtypeStruct((t, f), BF16),
        grid=(t // tm, f // tn),
        in_specs=[
            pl.BlockSpec((tm, d), lambda i, j: (i, 0)),
            pl.BlockSpec((None, 6, d), lambda i, j: (i // tps, 0, 0)),
            pl.BlockSpec((d, tn), lambda i, j: (0, j)),
            pl.BlockSpec((d, tn), lambda i, j: (0, j)),
        ],
        out_specs=pl.BlockSpec((tm, tn), lambda i, j: (i, j)),
        scratch_shapes=[pltpu.VMEM((tm, d), BF16)],
        compiler_params=_params("parallel", "arbitrary"),
        name="ffn_up",
    )(x2, mod_l, w_gate, w_up)


def _pad_cols(a, width):
    return jnp.pad(a, [(0, 0)] * (a.ndim - 1) + [(0, width - a.shape[-1])])


def _pad_rows(a, height):
    return jnp.pad(a, [(0, 0)] * (a.ndim - 2) + [(0, height - a.shape[-2]), (0, 0)])


def _rope_tables(seq):
    pos = jnp.arange(seq, dtype=F32)
    inv = ROPE_THETA ** (-jnp.arange(0, DIFF_HEAD, 2, dtype=F32) / DIFF_HEAD)
    ang = pos[:, None] * inv[None, :]
    emb = jnp.concatenate([ang, ang], axis=-1)
    sign = jnp.where(jnp.arange(DIFF_HEAD) < DIFF_HEAD // 2, -1.0, 1.0).astype(F32)
    return jnp.cos(emb), jnp.sin(emb) * sign[None, :]


def kernel(x, c, ada_w, ada_b, w_in, shift_mu_prev, shift_mu_next, decay_w0, decay_up, iclr_a0, iclr_up, gate_up, k_k, k_a, r_k, ln_x_w, ln_x_b, vres_down, vres_up, vres_v0, lambda_q1, lambda_k1, lambda_q2, lambda_k2, subln_w, proj_a, proj_b, w_out, ln1_g, ln1_b, ffn_w_gate, ffn_w_up, ffn_w_down, ln2_g, ln2_b):
    batch, seq, d = x.shape
    depth = ada_w.shape[0]
    t = batch * seq
    dl, il, gl_rank = decay_up.shape[2], iclr_up.shape[2], gate_up.shape[1]
    vl = vres_down.shape[2]
    assert max(dl, il, vl) <= LANES and gl_rank <= GROUP_LANES
    rwkv_cols = 3 * d + 2 * dl + 2 * il + gl_rank
    alpha = (2 * depth) ** 0.25

    mod = _modulation(c, ada_w, ada_b)
    cos, sin_signed = _rope_tables(seq)
    x2 = x.reshape(t, d)
    zs_first = None

    for l in range(depth):
        mod_l = mod[l]
        w_l = w_in[l]
        gw = _groups_per_step(d) * GROUP_LANES
        cuts = np.cumsum([3 * d, dl, dl, il, il, gl_rank])

        def regroup(a):
            rkv, xw_f, xw_b, xa_f, xa_b, xg = jnp.split(a[..., :rwkv_cols], cuts[:-1], axis=-1)
            lead = rkv.shape[:-1]
            rkv = jnp.swapaxes(rkv.reshape(*lead, 3, d // gw, gw), -3, -2).reshape(*lead, 3 * d)
            return [rkv, _pad_cols(xw_f, LANES), _pad_cols(xa_f, LANES), _pad_cols(xw_b, LANES),
                    _pad_cols(xa_b, LANES), _pad_cols(xg, GROUP_LANES)]

        w_parts = regroup(w_l)
        if l > 0:
            w_parts.append(_pad_cols(vres_down[l - 1], LANES))
        else:
            w_parts.append(jnp.zeros((d, LANES), F32))
        n_rwkv = sum(p.shape[1] for p in w_parts)
        n_pad = -(-n_rwkv // 512) * 512
        w_rwkv = _pad_cols(jnp.concatenate(w_parts, axis=1), n_pad).astype(BF16)
        mu = jnp.stack([
            _pad_cols(jnp.concatenate(regroup(shift_mu_prev[l]), axis=0), n_pad),
            _pad_cols(jnp.concatenate(regroup(shift_mu_next[l]), axis=0), n_pad)])
        w_att = w_l[:, rwkv_cols:].astype(BF16)

        zs = _inproj_shift(x2, mod_l, w_rwkv, mu, seq, F32)
        z_att = _inproj_plain(x2, mod_l, w_att, seq, BF16)

        vup = _pad_rows(vres_up[l - 1], LANES) if l > 0 else jnp.zeros((LANES, d), F32)
        vv0 = vres_v0[l - 1] if l > 0 else jnp.zeros((d,), F32)
        ups = jnp.concatenate([_pad_rows(decay_up[l], LANES), _pad_rows(iclr_up[l], LANES), vup[None]],
                              axis=0).astype(BF16)
        vecs = jnp.stack([decay_w0[l, 0], decay_w0[l, 1], iclr_a0[l, 0], iclr_a0[l, 1],
                          k_k[l], k_a[l], r_k[l].reshape(d), vv0])
        o_f, o_b, bon_f, bon_b = _wkv(zs, zs_first if l > 0 else None, ups, vecs, batch, seq, d)
        if l == 0:
            zs_first = zs
        y_a = _rwkv_out(o_f, o_b, bon_f, bon_b, zs, _pad_rows(gate_up[l], GROUP_LANES).astype(BF16),
                        ln_x_w[l].reshape(1, d), ln_x_b[l].reshape(1, d), d)

        lam_init = 0.8 - 0.6 * math.exp(-0.3 * l)
        lam_vecs = jnp.stack([lambda_q1[l], lambda_k1[l], lambda_q2[l], lambda_k2[l]])
        y_b = _diff_attention(z_att, cos, sin_signed, lam_vecs, subln_w[l].reshape(1, -1),
                              batch, seq, d, lam_init)

        merged = _merge(y_a, y_b, proj_a[l].astype(BF16), proj_b[l].astype(BF16), z_att, d)
        x2 = _mm_residual_ln(merged, w_out[l].astype(BF16), x2, mod_l, 2,
                             ln1_g[l].reshape(1, d), ln1_b[l].reshape(1, d), seq, alpha)

        h = _ffn_up(x2, mod_l, ffn_w_gate[l].astype(BF16), ffn_w_up[l].astype(BF16), seq)
        x2 = _mm_residual_ln(h, ffn_w_down[l].astype(BF16), x2, mod_l, 5,
                             ln2_g[l].reshape(1, d), ln2_b[l].reshape(1, d), seq, alpha)

    return x2.reshape(batch, seq, d)
```

```python
import functools
import math

import numpy as np
import jax
import jax.numpy as jnp
from jax import lax
from jax.experimental import pallas as pl
from jax.experimental.pallas import tpu as pltpu

F32 = jnp.float32
BF16 = jnp.bfloat16
HIGHEST = lax.Precision.HIGHEST

RWKV_HEAD = 64
DIFF_HEAD = 128
ROPE_THETA = 10000.0
SUBLN_EPS = 1e-5
LN_EPS = 1e-5
GN_EPS = 1e-5 * RWKV_HEAD
EXP_NEG_HALF = math.exp(-0.5)
LOG2E = math.log2(math.e)

LANES = 128
MXU_DIM = 256
VMEM_LIMIT = 56 * 1024 * 1024

CHUNK = 64
HEADS_PER_GROUP = MXU_DIM // RWKV_HEAD
GROUP_LANES = HEADS_PER_GROUP * RWKV_HEAD
STACK_ROWS = HEADS_PER_GROUP * CHUNK
MAX_GROUPS_PER_STEP = 4
HALO = 16


def _sigmoid(x):
    return 1.0 / (1.0 + jnp.exp(-x))


def _dot(a, b, precision=None):
    return jnp.dot(a, b, preferred_element_type=F32, precision=precision)


def _dot_nt(a, b):
    return lax.dot_general(a, b, (((1,), (1,)), ((), ())), preferred_element_type=F32)


def _dot_tn(a, b):
    return lax.dot_general(a, b, (((0,), (0,)), ((), ())), preferred_element_type=F32)


def _split2(x):
    hi = x.astype(BF16)
    return hi, (x - hi.astype(F32)).astype(BF16)


def _split3(x):
    hi = x.astype(BF16)
    rest = x - hi.astype(F32)
    mid = rest.astype(BF16)
    return hi, mid, (rest - mid.astype(F32)).astype(BF16)


def _params(*semantics):
    return pltpu.CompilerParams(dimension_semantics=semantics, vmem_limit_bytes=VMEM_LIMIT)


def _pick_tile(n, target, quantum):
    best = None
    t = quantum
    while t <= min(n, target):
        if n % t == 0:
            best = t
        t += quantum
    assert best is not None, (n, target, quantum)
    return best


def _mod_kernel(c_ref, w_ref, b_ref, o_ref):
    c = c_ref[...]
    c_act = c * _sigmoid(c)
    o_ref[...] = _dot(c_act, w_ref[...], HIGHEST) + b_ref[...]


def _modulation(c, ada_w, ada_b):
    nl, d, n6 = ada_w.shape
    b = c.shape[0]
    rows = -(-b // 8) * 8
    c_pad = jnp.pad(c, ((0, rows - b), (0, 0)))
    tn = _pick_tile(n6, 512, LANES)
    out = pl.pallas_call(
        _mod_kernel,
        out_shape=jax.ShapeDtypeStruct((nl, rows, n6), F32),
        grid=(nl, n6 // tn),
        in_specs=[
            pl.BlockSpec((rows, d), lambda l, j: (0, 0)),
            pl.BlockSpec((None, d, tn), lambda l, j: (l, 0, j)),
            pl.BlockSpec((None, 1, tn), lambda l, j: (l, 0, j)),
        ],
        out_specs=pl.BlockSpec((None, rows, tn), lambda l, j: (l, 0, j)),
        compiler_params=_params("parallel", "parallel"),
        name="adaln_mod",
    )(c_pad, ada_w, ada_b.reshape(nl, 1, n6))
    return out[:, :b].reshape(nl, b, 6, d)


def _inproj_kernel(x_ref, xp_ref, xn_ref, mod_ref, w_ref, mu_ref, o_ref, u_ref, *acc_refs,
                   tm, tiles_per_seq, shift_row, scale_row):
    i = pl.program_id(0)
    j = pl.program_id(1)

    @pl.when(j == 0)
    def _():
        sc = 1.0 + mod_ref[scale_row:scale_row + 1, :]
        sh = mod_ref[shift_row:shift_row + 1, :]
        u_ref[0:HALO, :] = (xp_ref[...] * sc + sh).astype(BF16)
        u_ref[HALO:HALO + tm, :] = (x_ref[...] * sc + sh).astype(BF16)
        u_ref[HALO + tm:, :] = (xn_ref[...] * sc + sh).astype(BF16)

    row = lax.broadcasted_iota(jnp.int32, (tm, 1), 0)
    pos = i % tiles_per_seq
    first = jnp.logical_and(row == 0, pos == 0)
    last = jnp.logical_and(row == tm - 1, pos == tiles_per_seq - 1)

    def shift_out(acc_ref, cols):
        zc = acc_ref[HALO:HALO + tm, :]
        zp = jnp.where(first, 0.0, acc_ref[HALO - 1:HALO - 1 + tm, :])
        zn = jnp.where(last, 0.0, acc_ref[HALO + 1:HALO + 1 + tm, :])
        o_ref[:, cols] = (zc + mu_ref[0:1, cols] * (zp - zc) + mu_ref[1:2, cols] * (zn - zc)).astype(o_ref.dtype)

    sub = acc_refs[0].shape[1]
    parts = [slice(p * sub, (p + 1) * sub) for p in range(len(acc_refs))]
    for p, cols in enumerate(parts):
        acc_refs[p][...] = _dot(u_ref[...], w_ref[:, cols])
        if p > 0:
            shift_out(acc_refs[p - 1], parts[p - 1])
    shift_out(acc_refs[-1], parts[-1])


def _plain_inproj_kernel(x_ref, mod_ref, w_ref, o_ref, u_ref, *, shift_row, scale_row):
    @pl.when(pl.program_id(1) == 0)
    def _():
        sc = 1.0 + mod_ref[scale_row:scale_row + 1, :]
        sh = mod_ref[shift_row:shift_row + 1, :]
        u_ref[...] = (x_ref[...] * sc + sh).astype(BF16)

    o_ref[...] = _dot(u_ref[...], w_ref[...]).astype(o_ref.dtype)


def _inproj_shift(x2, mod_l, w, mu, seq, out_dtype):
    t, d = x2.shape
    n = w.shape[1]
    tm = _pick_tile(seq, 1024, HALO)
    tn = _pick_tile(n, 1024, 2 * MXU_DIM)
    sub = 2 * MXU_DIM
    tps = seq // tm
    hb = tm // HALO
    last_halo = t // HALO - 1
    kern = functools.partial(_inproj_kernel, tm=tm, tiles_per_seq=tps, shift_row=0, scale_row=1)
    return pl.pallas_call(
        kern,
        out_shape=jax.ShapeDtypeStruct((t, n), out_dtype),
        grid=(t // tm, n // tn),
        in_specs=[
            pl.BlockSpec((tm, d), lambda i, j: (i, 0)),
            pl.BlockSpec((HALO, d), lambda i, j: (jnp.maximum(i * hb - 1, 0), 0)),
            pl.BlockSpec((HALO, d), lambda i, j: (jnp.minimum((i + 1) * hb, last_halo), 0)),
            pl.BlockSpec((None, 6, d), lambda i, j: (i // tps, 0, 0)),
            pl.BlockSpec((d, tn), lambda i, j: (0, j)),
            pl.BlockSpec((2, tn), lambda i, j: (0, j)),
        ],
        out_specs=pl.BlockSpec((tm, tn), lambda i, j: (i, j)),
        scratch_shapes=[pltpu.VMEM((tm + 2 * HALO, d), BF16)]
        + [pltpu.VMEM((tm + 2 * HALO, sub), F32) for _ in range(tn // sub)],
        compiler_params=_params("parallel", "arbitrary"),
        name="inproj_shift",
    )(x2, x2, x2, mod_l, w, mu)


def _inproj_plain(x2, mod_l, w, seq, out_dtype):
    t, d = x2.shape
    n = w.shape[1]
    tm = _pick_tile(seq, 1024, HALO)
    tn = _pick_tile(n, 1024, LANES)
    tps = seq // tm
    kern = functools.partial(_plain_inproj_kernel, shift_row=0, scale_row=1)
    return pl.pallas_call(
        kern,
        out_shape=jax.ShapeDtypeStruct((t, n), out_dtype),
        grid=(t // tm, n // tn),
        in_specs=[
            pl.BlockSpec((tm, d), lambda i, j: (i, 0)),
            pl.BlockSpec((None, 6, d), lambda i, j: (i // tps, 0, 0)),
            pl.BlockSpec((d, tn), lambda i, j: (0, j)),
        ],
        out_specs=pl.BlockSpec((tm, tn), lambda i, j: (i, j)),
        scratch_shapes=[pltpu.VMEM((tm, d), BF16)],
        compiler_params=_params("parallel", "arbitrary"),
        name="inproj_plain",
    )(x2, mod_l, w)


def _wkv_constants():
    c, gc, gl = CHUNK, STACK_ROWS, GROUP_LANES
    assert gc == gl
    t = np.arange(c)
    lower = t[None, :] <= t[:, None]
    col_t = np.arange(gc) % c
    row_h = np.arange(gc) // c
    lane_h = np.arange(gl) // RWKV_HEAD
    tri = np.zeros((gc, gc), bool)
    tri[0:c, 0:c] = lower
    tri[c:2 * c, 0:c] = lower.T
    bd_lanes = lane_h[:, None] == lane_h[None, :]
    slab_h = np.stack([row_h[:, None] == lane_h[None, :], row_h[:, None] == row_h[None, :], bd_lanes, tri])
    slab_f = np.concatenate([
        col_t[None, :] < t[:, None], col_t[None, :] > t[:, None],
        col_t[None, :] <= t[:, None], col_t[None, :] >= t[:, None],
        col_t[None, :] == t[:, None], bd_lanes])
    return jnp.asarray(slab_h, BF16), jnp.asarray(slab_f, F32)


def _groups_per_step(d):
    groups = d // GROUP_LANES
    return max(g for g in range(1, MAX_GROUPS_PER_STEP + 1) if groups % g == 0)


def _wkv_chain(d, gi, gps, zrkv, th, xab, vrb, v0, ups, vecs, cbf, cf, q_ref, o_ref, bon_ref, has_vres):
    c, g, gc, gl = CHUNK, HEADS_PER_GROUP, STACK_ROWS, GROUP_LANES
    w = gps * gl
    sl = slice(gi * gl, (gi + 1) * gl)
    r = zrkv[:, gi * gl:(gi + 1) * gl]
    k = zrkv[:, w + gi * gl:w + (gi + 1) * gl]
    v = zrkv[:, 2 * w + gi * gl:2 * w + (gi + 1) * gl]
    w0, a0 = vecs[d:d + 1, sl], vecs[2 + d:3 + d, sl]
    k_k, k_a, r_k = vecs[4:5, sl], vecs[5:6, sl], vecs[6:7, sl]
    sm, bdm, bdl = cbf[0], cbf[1], cbf[2]
    tri = cbf[3, d * c:(d + 1) * c, 0:c]
    ms = cf[d * c:(d + 1) * c, :] > 0.5
    mi = cf[(2 + d) * c:(3 + d) * c, :] > 0.5
    eye_w = cf[4 * c:5 * c, :]

    wl = _dot(th, ups[d, :, sl])
    al = _dot(xab, ups[2 + d, :, sl])
    if has_vres:
        ml = _dot(vrb, ups[4, :, sl])
    yield
    if has_vres:
        v = v + (v0[:, sl] - v) * _sigmoid(vecs[7:8, sl] + ml)
    lw = -EXP_NEG_HALF * _sigmoid(w0 + wl)
    a = _sigmoid(a0 + al)
    kd = k * (1.0 + (a - 1.0) * k_a)
    kk0 = k * k_k
    sums = _dot(jnp.concatenate([*_split2(kk0 * kk0), *_split2(r * kd * r_k)], axis=0), bdl)
    cum3 = _dot(tri, jnp.concatenate(_split3(lw), axis=1))
    yield
    kk = kk0 * lax.rsqrt(jnp.maximum(sums[0:c] + sums[c:2 * c], 1e-24))
    bon_ref[:, sl] = (sums[2 * c:3 * c] + sums[3 * c:4 * c]) * v
    cum = cum3[:, 0:gl] + cum3[:, gl:2 * gl] + cum3[:, 2 * gl:3 * gl]
    tot = jnp.sum(lw, axis=0, keepdims=True)
    g_inv = jnp.exp(-cum)
    b = kk * a
    a_t = -(kk * jnp.exp(cum - lw))
    b_t = b * g_inv
    k_t = kd * g_inv
    r_t = r * jnp.exp(cum)
    g_rem = jnp.exp(tot - cum)
    bk_g = jnp.concatenate([b * g_rem, kd * g_rem], axis=0).astype(BF16)

    def stack(x):
        return jnp.concatenate([x.astype(BF16)] * g, axis=0) * sm

    def to_bd(xw):
        return jnp.concatenate([xw.astype(BF16)] * g, axis=0) * bdm

    ar = jnp.concatenate([a_t, r_t], axis=0).astype(BF16)
    nn = _dot_nt(ar, jnp.concatenate([stack(b_t), stack(k_t)], axis=0))
    q = q_ref[...]
    arq = _dot_nt(ar, q.astype(BF16))
    yield
    n_ab = jnp.where(ms, nn[0:c, 0:gc], 0.0)
    n_ak = jnp.where(ms, nn[0:c, gc:2 * gc], 0.0)
    n_rb = jnp.where(mi, nn[c:2 * c, 0:gc], 0.0).astype(BF16)
    n_rk = jnp.where(mi, nn[c:2 * c, gc:2 * gc], 0.0)
    pw = _dot(n_ab.astype(BF16), to_bd(n_ab))
    nv = _dot(jnp.concatenate([n_ak.astype(BF16), n_rk.astype(BF16)], axis=0), stack(v))
    yield
    inv = eye_w + n_ab
    steps = int(math.log2(c)) - 1
    for s in range(steps):
        rhs = to_bd(pw)
        if s < steps - 1:
            both = _dot(jnp.concatenate([pw.astype(BF16), inv.astype(BF16)], axis=0), rhs)
            yield
            pw = both[0:c]
            inv = inv + both[c:2 * c]
        else:
            last = _dot(inv.astype(BF16), rhs)
            yield
            inv = inv + last
    y = arq[0:c] + nv[0:c]
    u = _dot(inv.astype(BF16), stack(y))
    yield
    o_u = _dot(n_rb, stack(u))
    upd = _dot_tn(jnp.concatenate([u, v], axis=0).astype(BF16), bk_g)
    yield
    o_ref[:, sl] = arq[c:2 * c] + nv[c:2 * c] + o_u
    q_ref[...] = (q * jnp.exp(tot) + upd) * cf[5 * c:5 * c + gl, :]


def _wkv_kernel(*refs, has_vres, gps):
    n_act = 4 if has_vres else 2
    acts = [list(refs[0:n_act]), list(refs[n_act:2 * n_act])]
    ups, vecs, cbf, cf = refs[2 * n_act:2 * n_act + 4]
    outs = refs[2 * n_act + 4:2 * n_act + 8]
    q_refs = refs[2 * n_act + 8]

    @pl.when(pl.program_id(2) == 0)
    def _():
        q_refs[...] = jnp.zeros_like(q_refs)

    chains = []
    for d in range(2):
        zrkv, xwa = acts[d][0:2]
        v0, vr = (acts[d][2], acts[d][3]) if has_vres else (None, None)
        th = jnp.tanh(xwa[:, 0:LANES]).astype(BF16)
        xab = xwa[:, LANES:2 * LANES].astype(BF16)
        vrb = vr[...].astype(BF16) if has_vres else None
        for gi in range(gps):
            chains.append(_wkv_chain(d, gi, gps, zrkv, th, xab, vrb, v0, ups, vecs, cbf, cf,
                                     q_refs.at[d * gps + gi], outs[d], outs[2 + d], has_vres))
    live = chains
    while live:
        live = [ch for ch in live if next(ch, True) is None]


def _wkv(zs, zs_first, ups, vecs, batch, seq, d):
    t = zs.shape[0]
    has_vres = zs_first is not None
    c, gl = CHUNK, GROUP_LANES
    gps = _groups_per_step(d)
    w = gps * gl
    assert seq % c == 0 and d % w == 0
    nc = seq // c
    ng = d // w
    lora_pair0 = 3 * d // (2 * LANES)
    vres_block = (3 * d + 4 * LANES + gl) // LANES

    def dir_specs(rev):
        rw = (lambda b, i: b * nc + (nc - 1 - i)) if rev else (lambda b, i: b * nc + i)
        specs = [
            pl.BlockSpec((c, 3 * w), lambda b, g, i: (rw(b, i), g)),
            pl.BlockSpec((c, 2 * LANES), lambda b, g, i: (rw(b, i), lora_pair0 + rev)),
        ]
        args = [zs, zs]
        if has_vres:
            specs += [
                pl.BlockSpec((c, w), lambda b, g, i: (rw(b, i), 3 * g + 2)),
                pl.BlockSpec((c, LANES), lambda b, g, i: (rw(b, i), vres_block)),
            ]
            args += [zs_first, zs]
        return specs, args

    fs, fa = dir_specs(0)
    bs, ba = dir_specs(1)
    cbf, cf = _wkv_constants()
    p_specs = [
        pl.BlockSpec((5, LANES, w), lambda b, g, i: (0, 0, g)),
        pl.BlockSpec((8, w), lambda b, g, i: (0, g)),
        pl.BlockSpec(cbf.shape, lambda b, g, i: (0, 0, 0)),
        pl.BlockSpec(cf.shape, lambda b, g, i: (0, 0)),
    ]
    out_f = pl.BlockSpec((c, w), lambda b, g, i: (b * nc + i, g))
    out_b = pl.BlockSpec((c, w), lambda b, g, i: (b * nc + (nc - 1 - i), g))
    shp = jax.ShapeDtypeStruct((t, d), F32)
    return pl.pallas_call(
        functools.partial(_wkv_kernel, has_vres=has_vres, gps=gps),
        out_shape=[shp, shp, shp, shp],
        grid=(batch, ng, nc),
        in_specs=fs + bs + p_specs,
        out_specs=[out_f, out_b, out_f, out_b],
        scratch_shapes=[pltpu.VMEM((2 * gps, gl, gl), F32)],
        compiler_params=_params("parallel", "parallel", "arbitrary"),
        name="wkv7_chunked",
    )(*fa, *ba, ups, vecs, cbf, cf)


def _rwkv_out_kernel(of_ref, ob_ref, bf_ref, bb_ref, xg_ref, gup_ref, lnw_ref, lnb_ref, bd_ref, o_ref):
    o = of_ref[...] + ob_ref[...]
    bd = bd_ref[...]
    inv_n = 1.0 / RWKV_HEAD
    tm = o.shape[0]

    def head_sum(x):
        s2 = _dot(jnp.concatenate(_split2(x), axis=0), bd)
        return s2[0:tm] + s2[tm:2 * tm]

    mu = head_sum(o) * inv_n
    oc = o - mu
    var = head_sum(oc * oc) * inv_n
    o_n = oc * lax.rsqrt(var + GN_EPS) * lnw_ref[...] + lnb_ref[...]
    gate = _dot(_sigmoid(xg_ref[...]).astype(BF16), gup_ref[...])
    o_ref[...] = ((o_n + bf_ref[...] + bb_ref[...]) * gate).astype(o_ref.dtype)


def _rwkv_out(o_f, o_b, bon_f, bon_b, zs, gate_up, ln_w, ln_b, d):
    t = o_f.shape[0]
    gl = GROUP_LANES
    tm = _pick_tile(t, 1024, 8)
    xg_block = (3 * d + 4 * LANES) // gl
    bd = _wkv_constants()[0][2]
    big = pl.BlockSpec((tm, gl), lambda i, j: (i, j))
    return pl.pallas_call(
        _rwkv_out_kernel,
        out_shape=jax.ShapeDtypeStruct((t, d), BF16),
        grid=(t // tm, d // gl),
        in_specs=[
            big, big, big, big,
            pl.BlockSpec((tm, gl), lambda i, j: (i, xg_block)),
            pl.BlockSpec((gl, gl), lambda i, j: (0, j)),
            pl.BlockSpec((1, gl), lambda i, j: (0, j)),
            pl.BlockSpec((1, gl), lambda i, j: (0, j)),
            pl.BlockSpec((gl, gl), lambda i, j: (0, 0)),
        ],
        out_specs=big,
        compiler_params=_params("parallel", "parallel"),
        name="rwkv_out",
    )(o_f, o_b, bon_f, bon_b, zs, gate_up, ln_w, ln_b, bd)


def _attn_kernel(q_ref, k_ref, v_ref, cos_ref, sin_ref, lam_ref, sw_ref, o_ref, kr_ref, *, tq, tk, lam_init):
    qi = pl.program_id(2)
    dh = DIFF_HEAD

    @pl.when(qi == 0)
    def _():
        blk = 512 if k_ref.shape[0] % 512 == 0 else k_ref.shape[0]
        for c0 in range(0, k_ref.shape[0], blk):
            cos = cos_ref[c0:c0 + blk, :]
            sin = sin_ref[c0:c0 + blk, :]
            for m in range(2):
                km = k_ref[c0:c0 + blk, m * dh:(m + 1) * dh].astype(F32)
                kr = km * cos + pltpu.roll(km, dh // 2, 1) * sin
                kr_ref[m, :, c0:c0 + blk] = kr.T.astype(BF16)

    lam = (jnp.exp(jnp.sum(lam_ref[0:1, :] * lam_ref[1:2, :], axis=-1, keepdims=True))
           - jnp.exp(jnp.sum(lam_ref[2:3, :] * lam_ref[3:4, :], axis=-1, keepdims=True)) + lam_init)
    row0 = pl.multiple_of(qi * tq, tq)
    cq = cos_ref[pl.ds(row0, tq), :]
    sq = sin_ref[pl.ds(row0, tq), :]
    qs = []
    for m in range(2):
        qm = q_ref[:, m * dh:(m + 1) * dh].astype(F32)
        qm = (qm * cq + pltpu.roll(qm, dh // 2, 1) * sq) * (dh ** -0.5 * LOG2E)
        qs.append(qm.astype(BF16))
    nkv = k_ref.shape[0] // tk

    def lane_fold(x, op):
        out = x[:, 0:LANES]
        for cidx in range(1, tk // LANES):
            out = op(out, x[:, cidx * LANES:(cidx + 1) * LANES])
        return out

    mx, lp, acc = [None, None], [None, None], [None, None]
    for j in range(nkv):
        vt = v_ref[j * tk:(j + 1) * tk, :]
        for m in range(2):
            s = _dot(qs[m], kr_ref[m, :, j * tk:(j + 1) * tk])
            tile_max = jnp.max(lane_fold(s, jnp.maximum), axis=-1, keepdims=True)
            m_new = tile_max if j == 0 else jnp.maximum(mx[m], tile_max)
            p = jnp.exp2(s - m_new)
            psum = lane_fold(p, jnp.add)
            pv = _dot(p.astype(BF16), vt)
            if j == 0:
                lp[m], acc[m] = psum, pv
            else:
                alpha = jnp.exp2(mx[m] - m_new)
                lp[m] = alpha * lp[m] + psum
                acc[m] = alpha * acc[m] + pv
            mx[m] = m_new
    l1 = jnp.sum(lp[0], axis=-1, keepdims=True)
    l2 = jnp.sum(lp[1], axis=-1, keepdims=True)
    o = acc[0] * (1.0 / l1) - acc[1] * (lam / l2)
    o = o * lax.rsqrt(jnp.mean(o * o, axis=-1, keepdims=True) + SUBLN_EPS) * sw_ref[...]
    o_ref[...] = (o * (1.0 - lam_init)).astype(o_ref.dtype)


def _diff_attention(z_att, cos, sin_signed, lam_vecs, subln_w, batch, seq, d, lam_init):
    t = z_att.shape[0]
    hw = 2 * DIFF_HEAD
    nh = d // hw
    tq = _pick_tile(seq, 256, 8)
    nq = seq // tq
    tk = _pick_tile(seq, MXU_DIM, LANES)
    kern = functools.partial(_attn_kernel, tq=tq, tk=tk, lam_init=lam_init)
    return pl.pallas_call(
        kern,
        out_shape=jax.ShapeDtypeStruct((t, d), BF16),
        grid=(batch, nh, nq),
        in_specs=[
            pl.BlockSpec((tq, hw), lambda b, h, i: (b * nq + i, h)),
            pl.BlockSpec((seq, hw), lambda b, h, i: (b, nh + h)),
            pl.BlockSpec((seq, hw), lambda b, h, i: (b, 2 * nh + h)),
            pl.BlockSpec((seq, DIFF_HEAD), lambda b, h, i: (0, 0)),
            pl.BlockSpec((seq, DIFF_HEAD), lambda b, h, i: (0, 0)),
            pl.BlockSpec((4, DIFF_HEAD), lambda b, h, i: (0, 0)),
            pl.BlockSpec((1, hw), lambda b, h, i: (0, 0)),
        ],
        out_specs=pl.BlockSpec((tq, hw), lambda b, h, i: (b * nq + i, h)),
        scratch_shapes=[pltpu.VMEM((2, DIFF_HEAD, seq), BF16)],
        compiler_params=_params("parallel", "parallel", "arbitrary"),
        name="diff_attention",
    )(z_att, z_att, z_att, cos, sin_signed, lam_vecs, subln_w)


def _merge_kernel(ya_ref, yb_ref, pa_ref, pb_ref, ga_ref, gb_ref, o_ref):
    ma = _dot(ya_ref[...], pa_ref[...])
    mb = _dot(yb_ref[...], pb_ref[...])
    ga = _sigmoid(ga_ref[...].astype(F32))
    gb = _sigmoid(gb_ref[...].astype(F32))
    o_ref[...] = (ga * ma + gb * mb).astype(o_ref.dtype)


def _merge(y_a, y_b, proj_a, proj_b, z_att, d):
    t = y_a.shape[0]
    tm = _pick_tile(t, 1024, 8)
    tn = _pick_tile(d, 512, LANES)
    nb = d // tn
    return pl.pallas_call(
        _merge_kernel,
        out_shape=jax.ShapeDtypeStruct((t, d), BF16),
        grid=(t // tm, nb),
        in_specs=[
            pl.BlockSpec((tm, d), lambda i, j: (i, 0)),
            pl.BlockSpec((tm, d), lambda i, j: (i, 0)),
            pl.BlockSpec((d, tn), lambda i, j: (0, j)),
            pl.BlockSpec((d, tn), lambda i, j: (0, j)),
            pl.BlockSpec((tm, tn), lambda i, j: (i, 3 * nb + j)),
            pl.BlockSpec((tm, tn), lambda i, j: (i, 4 * nb + j)),
        ],
        out_specs=pl.BlockSpec((tm, tn), lambda i, j: (i, j)),
        compiler_params=_params("parallel", "arbitrary"),
        name="gated_merge",
    )(y_a, y_b, proj_a, proj_b, z_att, z_att)


def _mm_ln_kernel(a_ref, w_ref, x_ref, mod_ref, g_ref, b_ref, o_ref, *, nk, gate_row, alpha):
    kk = pl.program_id(1)
    tm = o_ref.shape[0]
    halves = [slice(0, tm // 2), slice(tm // 2, tm)]

    def layer_norm_rows(rows, acc):
        y = alpha * x_ref[rows, :] + mod_ref[gate_row:gate_row + 1, :] * acc
        mu = jnp.mean(y, axis=-1, keepdims=True)
        yc = y - mu
        var = jnp.mean(yc * yc, axis=-1, keepdims=True)
        o_ref[rows, :] = yc * lax.rsqrt(var + LN_EPS) * g_ref[...] + b_ref[...]

    def finish(accumulated):
        w = w_ref[...]
        acc0 = _dot(a_ref[halves[0], :], w)
        acc1 = _dot(a_ref[halves[1], :], w)
        if accumulated:
            acc0 = acc0 + o_ref[halves[0], :]
        layer_norm_rows(halves[0], acc0)
        if accumulated:
            acc1 = acc1 + o_ref[halves[1], :]
        layer_norm_rows(halves[1], acc1)

    if nk == 1:
        finish(False)
        return

    @pl.when(kk == 0)
    def _():
        o_ref[...] = _dot(a_ref[...], w_ref[...])

    @pl.when(jnp.logical_and(kk > 0, kk < nk - 1))
    def _():
        o_ref[...] += _dot(a_ref[...], w_ref[...])

    @pl.when(kk == nk - 1)
    def _():
        finish(True)


def _mm_residual_ln(a, w, x2, mod_l, gate_row, ln_g, ln_b, seq, alpha):
    t, kdim = a.shape
    d = w.shape[1]
    if kdim <= 2048:
        tm, tk = _pick_tile(seq, 512, 8), kdim
    else:
        tm, tk = _pick_tile(seq, 1024, 8), _pick_tile(kdim, 768, LANES)
    nk = kdim // tk
    tps = seq // tm
    kern = functools.partial(_mm_ln_kernel, nk=nk, gate_row=gate_row, alpha=alpha)
    return pl.pallas_call(
        kern,
        out_shape=jax.ShapeDtypeStruct((t, d), F32),
        grid=(t // tm, nk),
        in_specs=[
            pl.BlockSpec((tm, tk), lambda i, k: (i, k)),
            pl.BlockSpec((tk, d), lambda i, k: (k, 0)),
            pl.BlockSpec((tm, d), lambda i, k: (i, 0)),
            pl.BlockSpec((None, 6, d), lambda i, k: (i // tps, 0, 0)),
            pl.BlockSpec((1, d), lambda i, k: (0, 0)),
            pl.BlockSpec((1, d), lambda i, k: (0, 0)),
        ],
        out_specs=pl.BlockSpec((tm, d), lambda i, k: (i, 0)),
        compiler_params=_params("parallel", "arbitrary"),
        name="matmul_residual_ln",
    )(a, w, x2, mod_l, ln_g, ln_b)


def _ffn_up_kernel(x_ref, mod_ref, wg_ref, wu_ref, o_ref, u_ref, *, shift_row, scale_row):
    @pl.when(pl.program_id(1) == 0)
    def _():
        sc = 1.0 + mod_ref[scale_row:scale_row + 1, :]
        sh = mod_ref[shift_row:shift_row + 1, :]
        u_ref[...] = (x_ref[...] * sc + sh).astype(BF16)

    u = u_ref[...]
    hg = _dot(u, wg_ref[...])
    hu = _dot(u, wu_ref[...])
    o_ref[...] = (hg * _sigmoid(hg) * hu).astype(o_ref.dtype)


def _ffn_up(x2, mod_l, w_gate, w_up, seq):
    t, d = x2.shape
    f = w_gate.shape[1]
    tm = _pick_tile(seq, 1024, 8)
    tn = _pick_tile(f, 512, LANES)
    tps = seq // tm
    kern = functools.partial(_ffn_up_kernel, shift_row=3, scale_row=4)
    return pl.pallas_call(
        kern,
        out_shape=jax.ShapeDtypeStruct((t, f), BF16),
        grid=(t // tm, f // tn),
        in_specs=[
            pl.BlockSpec((tm, d), lambda i, j: (i, 0)),
            pl.BlockSpec((None, 6, d), lambda i, j: (i // tps, 0, 0)),
            pl.BlockSpec((d, tn), lambda i, j: (0, j)),
            pl.BlockSpec((d, tn), lambda i, j: (0, j)),
        ],
        out_specs=pl.BlockSpec((tm, tn), lambda i, j: (i, j)),
        scratch_shapes=[pltpu.VMEM((tm, d), BF16)],
        compiler_params=_params("parallel", "arbitrary"),
        name="ffn_up",
    )(x2, mod_l, w_gate, w_up)


def _pad_cols(a, width):
    return jnp.pad(a, [(0, 0)] * (a.ndim - 1) + [(0, width - a.shape[-1])])


def _pad_rows(a, height):
    return jnp.pad(a, [(0, 0)] * (a.ndim - 2) + [(0, height - a.shape[-2]), (0, 0)])


def _rope_tables(seq):
    pos = jnp.arange(seq, dtype=F32)
    inv = ROPE_THETA ** (-jnp.arange(0, DIFF_HEAD, 2, dtype=F32) / DIFF_HEAD)
    ang = pos[:, None] * inv[None, :]
    emb = jnp.concatenate([ang, ang], axis=-1)
    sign = jnp.where(jnp.arange(DIFF_HEAD) < DIFF_HEAD // 2, -1.0, 1.0).astype(F32)
    return jnp.cos(emb), jnp.sin(emb) * sign[None, :]


def kernel(x, c, ada_w, ada_b, w_in, shift_mu_prev, shift_mu_next, decay_w0, decay_up, iclr_a0, iclr_up, gate_up, k_k, k_a, r_k, ln_x_w, ln_x_b, vres_down, vres_up, vres_v0, lambda_q1, lambda_k1, lambda_q2, lambda_k2, subln_w, proj_a, proj_b, w_out, ln1_g, ln1_b, ffn_w_gate, ffn_w_up, ffn_w_down, ln2_g, ln2_b):
    batch, seq, d = x.shape
    depth = ada_w.shape[0]
    t = batch * seq
    dl, il, gl_rank = decay_up.shape[2], iclr_up.shape[2], gate_up.shape[1]
    vl = vres_down.shape[2]
    assert max(dl, il, vl) <= LANES and gl_rank <= GROUP_LANES
    rwkv_cols = 3 * d + 2 * dl + 2 * il + gl_rank
    alpha = (2 * depth) ** 0.25

    mod = _modulation(c, ada_w, ada_b)
    cos, sin_signed = _rope_tables(seq)
    x2 = x.reshape(t, d)
    zs_first = None

    for l in range(depth):
        mod_l = mod[l]
        w_l = w_in[l]
        gw = _groups_per_step(d) * GROUP_LANES
        cuts = np.cumsum([3 * d, dl, dl, il, il, gl_rank])

        def regroup(a):
            rkv, xw_f, xw_b, xa_f, xa_b, xg = jnp.split(a[..., :rwkv_cols], cuts[:-1], axis=-1)
            lead = rkv.shape[:-1]
            rkv = jnp.swapaxes(rkv.reshape(*lead, 3, d // gw, gw), -3, -2).reshape(*lead, 3 * d)
            return [rkv, _pad_cols(xw_f, LANES), _pad_cols(xa_f, LANES), _pad_cols(xw_b, LANES),
                    _pad_cols(xa_b, LANES), _pad_cols(xg, GROUP_LANES)]

        w_lb = w_l.astype(BF16)
        w_parts = regroup(w_lb)
        if l > 0:
            w_parts.append(_pad_cols(vres_down[l - 1].astype(BF16), LANES))
        else:
            w_parts.append(jnp.zeros((d, LANES), BF16))
        n_rwkv = sum(p.shape[1] for p in w_parts)
        n_pad = -(-n_rwkv // 1024) * 1024
        w_rwkv = _pad_cols(jnp.concatenate(w_parts, axis=1), n_pad)
        mu = jnp.stack([
            _pad_cols(jnp.concatenate(regroup(shift_mu_prev[l]), axis=0), n_pad),
            _pad_cols(jnp.concatenate(regroup(shift_mu_next[l]), axis=0), n_pad)])
        w_att = w_lb[:, rwkv_cols:]

        zs = _inproj_shift(x2, mod_l, w_rwkv, mu, seq, F32)
        z_att = _inproj_plain(x2, mod_l, w_att, seq, BF16)

        vup = _pad_rows(vres_up[l - 1], LANES) if l > 0 else jnp.zeros((LANES, d), F32)
        vv0 = vres_v0[l - 1] if l > 0 else jnp.zeros((d,), F32)
        ups = jnp.concatenate([_pad_rows(decay_up[l], LANES), _pad_rows(iclr_up[l], LANES), vup[None]],
                              axis=0).astype(BF16)
        vecs = jnp.stack([decay_w0[l, 0], decay_w0[l, 1], iclr_a0[l, 0], iclr_a0[l, 1],
                          k_k[l], k_a[l], r_k[l].reshape(d), vv0])
        o_f, o_b, bon_f, bon_b = _wkv(zs, zs_first if l > 0 else None, ups, vecs, batch, seq, d)
        if l == 0:
            zs_first = zs
        y_a = _rwkv_out(o_f, o_b, bon_f, bon_b, zs, _pad_rows(gate_up[l], GROUP_LANES).astype(BF16),
                        ln_x_w[l].reshape(1, d), ln_x_b[l].reshape(1, d), d)

        lam_init = 0.8 - 0.6 * math.exp(-0.3 * l)
        lam_vecs = jnp.stack([lambda_q1[l], lambda_k1[l], lambda_q2[l], lambda_k2[l]])
        y_b = _diff_attention(z_att, cos, sin_signed, lam_vecs, subln_w[l].reshape(1, -1),
                              batch, seq, d, lam_init)

        merged = _merge(y_a, y_b, proj_a[l].astype(BF16), proj_b[l].astype(BF16), z_att, d)
        x2 = _mm_residual_ln(merged, w_out[l].astype(BF16), x2, mod_l, 2,
                             ln1_g[l].reshape(1, d), ln1_b[l].reshape(1, d), seq, alpha)

        h = _ffn_up(x2, mod_l, ffn_w_gate[l].astype(BF16), ffn_w_up[l].astype(BF16), seq)
        x2 = _mm_residual_ln(h, ffn_w_down[l].astype(BF16), x2, mod_l, 5,
                             ln2_g[l].reshape(1, d), ln2_b[l].reshape(1, d), seq, alpha)

    return x2.reshape(batch, seq, d)
```

```python
import functools
import math

import numpy as np
import jax
import jax.numpy as jnp
from jax import lax
from jax.experimental import pallas as pl
from jax.experimental.pallas import tpu as pltpu

F32 = jnp.float32
BF16 = jnp.bfloat16
HIGHEST = lax.Precision.HIGHEST

RWKV_HEAD = 64
DIFF_HEAD = 128
ROPE_THETA = 10000.0
SUBLN_EPS = 1e-5
LN_EPS = 1e-5
GN_EPS = 1e-5 * RWKV_HEAD
EXP_NEG_HALF = math.exp(-0.5)
LOG2E = math.log2(math.e)

LANES = 128
MXU_DIM = 256
VMEM_LIMIT = 56 * 1024 * 1024

CHUNK = 64
HEADS_PER_GROUP = MXU_DIM // RWKV_HEAD
GROUP_LANES = HEADS_PER_GROUP * RWKV_HEAD
STACK_ROWS = HEADS_PER_GROUP * CHUNK
MAX_GROUPS_PER_STEP = 4
HALO = 16


def _sigmoid(x):
    return 1.0 / (1.0 + jnp.exp(-x))


def _dot(a, b, precision=None):
    return jnp.dot(a, b, preferred_element_type=F32, precision=precision)


def _dot_nt(a, b):
    return lax.dot_general(a, b, (((1,), (1,)), ((), ())), preferred_element_type=F32)


def _dot_tn(a, b):
    return lax.dot_general(a, b, (((0,), (0,)), ((), ())), preferred_element_type=F32)


def _split2(x):
    hi = x.astype(BF16)
    return hi, (x - hi.astype(F32)).astype(BF16)


def _split3(x):
    hi = x.astype(BF16)
    rest = x - hi.astype(F32)
    mid = rest.astype(BF16)
    return hi, mid, (rest - mid.astype(F32)).astype(BF16)


def _params(*semantics):
    return pltpu.CompilerParams(dimension_semantics=semantics, vmem_limit_bytes=VMEM_LIMIT)


def _pick_tile(n, target, quantum):
    best = None
    t = quantum
    while t <= min(n, target):
        if n % t == 0:
            best = t
        t += quantum
    assert best is not None, (n, target, quantum)
    return best


def _mod_kernel(c_ref, w_ref, b_ref, o_ref):
    c = c_ref[...]
    c_act = c * _sigmoid(c)
    o_ref[...] = _dot(c_act, w_ref[...], HIGHEST) + b_ref[...]


def _modulation(c, ada_w, ada_b):
    nl, d, n6 = ada_w.shape
    b = c.shape[0]
    rows = -(-b // 8) * 8
    c_pad = jnp.pad(c, ((0, rows - b), (0, 0)))
    tn = _pick_tile(n6, 512, LANES)
    out = pl.pallas_call(
        _mod_kernel,
        out_shape=jax.ShapeDtypeStruct((nl, rows, n6), F32),
        grid=(nl, n6 // tn),
        in_specs=[
            pl.BlockSpec((rows, d), lambda l, j: (0, 0)),
            pl.BlockSpec((None, d, tn), lambda l, j: (l, 0, j)),
            pl.BlockSpec((None, 1, tn), lambda l, j: (l, 0, j)),
        ],
        out_specs=pl.BlockSpec((None, rows, tn), lambda l, j: (l, 0, j)),
        compiler_params=_params("parallel", "parallel"),
        name="adaln_mod",
    )(c_pad, ada_w, ada_b.reshape(nl, 1, n6))
    return out[:, :b].reshape(nl, b, 6, d)


def _inproj_kernel(x_ref, xp_ref, xn_ref, mod_ref, w_ref, mu_ref, o_ref, u_ref, *acc_refs,
                   tm, tiles_per_seq, shift_row, scale_row):
    i = pl.program_id(0)
    j = pl.program_id(1)

    @pl.when(j == 0)
    def _():
        sc = 1.0 + mod_ref[scale_row:scale_row + 1, :]
        sh = mod_ref[shift_row:shift_row + 1, :]
        u_ref[0:HALO, :] = (xp_ref[...] * sc + sh).astype(BF16)
        u_ref[HALO:HALO + tm, :] = (x_ref[...] * sc + sh).astype(BF16)
        u_ref[HALO + tm:, :] = (xn_ref[...] * sc + sh).astype(BF16)

    row = lax.broadcasted_iota(jnp.int32, (tm, 1), 0)
    pos = i % tiles_per_seq
    first = jnp.logical_and(row == 0, pos == 0)
    last = jnp.logical_and(row == tm - 1, pos == tiles_per_seq - 1)

    def shift_out(acc_ref, cols):
        zc = acc_ref[HALO:HALO + tm, :]
        zp = jnp.where(first, 0.0, acc_ref[HALO - 1:HALO - 1 + tm, :])
        zn = jnp.where(last, 0.0, acc_ref[HALO + 1:HALO + 1 + tm, :])
        o_ref[:, cols] = (zc + mu_ref[0:1, cols] * (zp - zc) + mu_ref[1:2, cols] * (zn - zc)).astype(o_ref.dtype)

    sub = acc_refs[0].shape[1]
    parts = [slice(p * sub, (p + 1) * sub) for p in range(len(acc_refs))]
    for p, cols in enumerate(parts):
        acc_refs[p][...] = _dot(u_ref[...], w_ref[:, cols])
        if p > 0:
            shift_out(acc_refs[p - 1], parts[p - 1])
    shift_out(acc_refs[-1], parts[-1])


def _plain_inproj_kernel(x_ref, mod_ref, w_ref, o_ref, u_ref, *, shift_row, scale_row):
    @pl.when(pl.program_id(1) == 0)
    def _():
        sc = 1.0 + mod_ref[scale_row:scale_row + 1, :]
        sh = mod_ref[shift_row:shift_row + 1, :]
        u_ref[...] = (x_ref[...] * sc + sh).astype(BF16)

    o_ref[...] = _dot(u_ref[...], w_ref[...]).astype(o_ref.dtype)


def _inproj_shift(x2, mod_l, w, mu, seq, out_dtype):
    t, d = x2.shape
    n = w.shape[1]
    tm = _pick_tile(seq, 1024, HALO)
    tn = _pick_tile(n, 1024, 2 * MXU_DIM)
    sub = 2 * MXU_DIM
    tps = seq // tm
    hb = tm // HALO
    last_halo = t // HALO - 1
    kern = functools.partial(_inproj_kernel, tm=tm, tiles_per_seq=tps, shift_row=0, scale_row=1)
    return pl.pallas_call(
        kern,
        out_shape=jax.ShapeDtypeStruct((t, n), out_dtype),
        grid=(t // tm, n // tn),
        in_specs=[
            pl.BlockSpec((tm, d), lambda i, j: (i, 0)),
            pl.BlockSpec((HALO, d), lambda i, j: (jnp.maximum(i * hb - 1, 0), 0)),
            pl.BlockSpec((HALO, d), lambda i, j: (jnp.minimum((i + 1) * hb, last_halo), 0)),
            pl.BlockSpec((None, 6, d), lambda i, j: (i // tps, 0, 0)),
            pl.BlockSpec((d, tn), lambda i, j: (0, j)),
            pl.BlockSpec((2, tn), lambda i, j: (0, j)),
        ],
        out_specs=pl.BlockSpec((tm, tn), lambda i, j: (i, j)),
        scratch_shapes=[pltpu.VMEM((tm + 2 * HALO, d), BF16)]
        + [pltpu.VMEM((tm + 2 * HALO, sub), F32) for _ in range(tn // sub)],
        compiler_params=_params("parallel", "arbitrary"),
        name="inproj_shift",
    )(x2, x2, x2, mod_l, w, mu)


def _inproj_plain(x2, mod_l, w, seq, out_dtype):
    t, d = x2.shape
    n = w.shape[1]
    tm = _pick_tile(seq, 1024, HALO)
    tn = _pick_tile(n, 1024, LANES)
    tps = seq // tm
    kern = functools.partial(_plain_inproj_kernel, shift_row=0, scale_row=1)
    return pl.pallas_call(
        kern,
        out_shape=jax.ShapeDtypeStruct((t, n), out_dtype),
        grid=(t // tm, n // tn),
        in_specs=[
            pl.BlockSpec((tm, d), lambda i, j: (i, 0)),
            pl.BlockSpec((None, 6, d), lambda i, j: (i // tps, 0, 0)),
            pl.BlockSpec((d, tn), lambda i, j: (0, j)),
        ],
        out_specs=pl.BlockSpec((tm, tn), lambda i, j: (i, j)),
        scratch_shapes=[pltpu.VMEM((tm, d), BF16)],
        compiler_params=_params("parallel", "arbitrary"),
        name="inproj_plain",
    )(x2, mod_l, w)


def _wkv_constants():
    c, gc, gl = CHUNK, STACK_ROWS, GROUP_LANES
    assert gc == gl
    t = np.arange(c)
    lower = t[None, :] <= t[:, None]
    col_t = np.arange(gc) % c
    row_h = np.arange(gc) // c
    lane_h = np.arange(gl) // RWKV_HEAD
    tri = np.zeros((gc, gc), bool)
    tri[0:c, 0:c] = lower
    tri[c:2 * c, 0:c] = lower.T
    bd_lanes = lane_h[:, None] == lane_h[None, :]
    slab_h = np.stack([row_h[:, None] == lane_h[None, :], row_h[:, None] == row_h[None, :], bd_lanes, tri])
    slab_f = np.concatenate([
        col_t[None, :] < t[:, None], col_t[None, :] > t[:, None],
        col_t[None, :] <= t[:, None], col_t[None, :] >= t[:, None],
        col_t[None, :] == t[:, None], bd_lanes])
    return jnp.asarray(slab_h, BF16), jnp.asarray(slab_f, F32)


def _groups_per_step(d):
    groups = d // GROUP_LANES
    return max(g for g in range(1, MAX_GROUPS_PER_STEP + 1) if groups % g == 0)


def _wkv_chain(d, gi, gps, zrkv, th, xab, vrb, v0, ups, vecs, cbf, cf, q_ref, o_ref, bon_ref, has_vres):
    c, g, gc, gl = CHUNK, HEADS_PER_GROUP, STACK_ROWS, GROUP_LANES
    w = gps * gl
    sl = slice(gi * gl, (gi + 1) * gl)
    r = zrkv[:, gi * gl:(gi + 1) * gl]
    k = zrkv[:, w + gi * gl:w + (gi + 1) * gl]
    v = zrkv[:, 2 * w + gi * gl:2 * w + (gi + 1) * gl]
    w0, a0 = vecs[d:d + 1, sl], vecs[2 + d:3 + d, sl]
    k_k, k_a, r_k = vecs[4:5, sl], vecs[5:6, sl], vecs[6:7, sl]
    sm, bdm, bdl = cbf[0], cbf[1], cbf[2]
    tri = cbf[3, d * c:(d + 1) * c, 0:c]
    ms = cf[d * c:(d + 1) * c, :] > 0.5
    mi = cf[(2 + d) * c:(3 + d) * c, :] > 0.5
    eye_w = cf[4 * c:5 * c, :]

    wl = _dot(th, ups[d, :, sl])
    al = _dot(xab, ups[2 + d, :, sl])
    if has_vres:
        ml = _dot(vrb, ups[4, :, sl])
    yield
    if has_vres:
        v = v + (v0[:, sl] - v) * _sigmoid(vecs[7:8, sl] + ml)
    lw = -EXP_NEG_HALF * _sigmoid(w0 + wl)
    a = _sigmoid(a0 + al)
    kd = k * (1.0 + (a - 1.0) * k_a)
    kk0 = k * k_k
    sums = _dot(jnp.concatenate([*_split2(kk0 * kk0), *_split2(r * kd * r_k)], axis=0), bdl)
    cum3 = _dot(tri, jnp.concatenate(_split3(lw), axis=1))
    yield
    kk = kk0 * lax.rsqrt(jnp.maximum(sums[0:c] + sums[c:2 * c], 1e-24))
    bon_ref[:, sl] = ((sums[2 * c:3 * c] + sums[3 * c:4 * c]) * v).astype(bon_ref.dtype)
    cum = cum3[:, 0:gl] + cum3[:, gl:2 * gl] + cum3[:, 2 * gl:3 * gl]
    tot = jnp.sum(lw, axis=0, keepdims=True)
    g_inv = jnp.exp(-cum)
    b = kk * a
    a_t = -(kk * jnp.exp(cum - lw))
    b_t = b * g_inv
    k_t = kd * g_inv
    r_t = r * jnp.exp(cum)
    g_rem = jnp.exp(tot - cum)
    bk_g = jnp.concatenate([b * g_rem, kd * g_rem], axis=0).astype(BF16)

    def stack(x):
        return jnp.concatenate([x.astype(BF16)] * g, axis=0) * sm

    def to_bd(xw):
        return jnp.concatenate([xw.astype(BF16)] * g, axis=0) * bdm

    ar = jnp.concatenate([a_t, r_t], axis=0).astype(BF16)
    nn = _dot_nt(ar, jnp.concatenate([stack(b_t), stack(k_t)], axis=0))
    q = q_ref[...]
    arq = _dot_nt(ar, q.astype(BF16))
    yield
    n_ab = jnp.where(ms, nn[0:c, 0:gc], 0.0)
    n_ak = jnp.where(ms, nn[0:c, gc:2 * gc], 0.0)
    n_rb = jnp.where(mi, nn[c:2 * c, 0:gc], 0.0).astype(BF16)
    n_rk = jnp.where(mi, nn[c:2 * c, gc:2 * gc], 0.0)
    pw = _dot(n_ab.astype(BF16), to_bd(n_ab))
    nv = _dot(jnp.concatenate([n_ak.astype(BF16), n_rk.astype(BF16)], axis=0), stack(v))
    yield
    inv = eye_w + n_ab
    steps = int(math.log2(c)) - 1
    for s in range(steps):
        rhs = to_bd(pw)
        if s < steps - 1:
            both = _dot(jnp.concatenate([pw.astype(BF16), inv.astype(BF16)], axis=0), rhs)
            yield
            pw = both[0:c]
            inv = inv + both[c:2 * c]
        else:
            last = _dot(inv.astype(BF16), rhs)
            yield
            inv = inv + last
    y = arq[0:c] + nv[0:c]
    u = _dot(inv.astype(BF16), stack(y))
    yield
    o_u = _dot(n_rb, stack(u))
    upd = _dot_tn(jnp.concatenate([u, v], axis=0).astype(BF16), bk_g)
    yield
    o_ref[:, sl] = (arq[c:2 * c] + nv[c:2 * c] + o_u).astype(o_ref.dtype)
    q_ref[...] = (q * jnp.exp(tot) + upd) * cf[5 * c:5 * c + gl, :]


def _wkv_kernel(*refs, has_vres, gps):
    n_act = 4 if has_vres else 2
    acts = [list(refs[0:n_act]), list(refs[n_act:2 * n_act])]
    ups, vecs, cbf, cf = refs[2 * n_act:2 * n_act + 4]
    outs = refs[2 * n_act + 4:2 * n_act + 8]
    q_refs = refs[2 * n_act + 8]

    @pl.when(pl.program_id(2) == 0)
    def _():
        q_refs[...] = jnp.zeros_like(q_refs)

    chains = []
    for d in range(2):
        zrkv, xwa = acts[d][0:2]
        v0, vr = (acts[d][2], acts[d][3]) if has_vres else (None, None)
        th = jnp.tanh(xwa[:, 0:LANES]).astype(BF16)
        xab = xwa[:, LANES:2 * LANES].astype(BF16)
        vrb = vr[...].astype(BF16) if has_vres else None
        for gi in range(gps):
            chains.append(_wkv_chain(d, gi, gps, zrkv, th, xab, vrb, v0, ups, vecs, cbf, cf,
                                     q_refs.at[d * gps + gi], outs[d], outs[2 + d], has_vres))
    live = chains
    while live:
        live = [ch for ch in live if next(ch, True) is None]


def _wkv(zs, zs_first, ups, vecs, batch, seq, d):
    t = zs.shape[0]
    has_vres = zs_first is not None
    c, gl = CHUNK, GROUP_LANES
    gps = _groups_per_step(d)
    w = gps * gl
    assert seq % c == 0 and d % w == 0
    nc = seq // c
    ng = d // w
    lora_pair0 = 3 * d // (2 * LANES)
    vres_block = (3 * d + 4 * LANES + gl) // LANES

    def dir_specs(rev):
        rw = (lambda b, i: b * nc + (nc - 1 - i)) if rev else (lambda b, i: b * nc + i)
        specs = [
            pl.BlockSpec((c, 3 * w), lambda b, g, i: (rw(b, i), g)),
            pl.BlockSpec((c, 2 * LANES), lambda b, g, i: (rw(b, i), lora_pair0 + rev)),
        ]
        args = [zs, zs]
        if has_vres:
            specs += [
                pl.BlockSpec((c, w), lambda b, g, i: (rw(b, i), 3 * g + 2)),
                pl.BlockSpec((c, LANES), lambda b, g, i: (rw(b, i), vres_block)),
            ]
            args += [zs_first, zs]
        return specs, args

    fs, fa = dir_specs(0)
    bs, ba = dir_specs(1)
    cbf, cf = _wkv_constants()
    p_specs = [
        pl.BlockSpec((5, LANES, w), lambda b, g, i: (0, 0, g)),
        pl.BlockSpec((8, w), lambda b, g, i: (0, g)),
        pl.BlockSpec(cbf.shape, lambda b, g, i: (0, 0, 0)),
        pl.BlockSpec(cf.shape, lambda b, g, i: (0, 0)),
    ]
    out_f = pl.BlockSpec((c, w), lambda b, g, i: (b * nc + i, g))
    out_b = pl.BlockSpec((c, w), lambda b, g, i: (b * nc + (nc - 1 - i), g))
    shp = jax.ShapeDtypeStruct((t, d), BF16)
    return pl.pallas_call(
        functools.partial(_wkv_kernel, has_vres=has_vres, gps=gps),
        out_shape=[shp, shp, shp, shp],
        grid=(batch, ng, nc),
        in_specs=fs + bs + p_specs,
        out_specs=[out_f, out_b, out_f, out_b],
        scratch_shapes=[pltpu.VMEM((2 * gps, gl, gl), F32)],
        compiler_params=_params("parallel", "parallel", "arbitrary"),
        name="wkv7_chunked",
    )(*fa, *ba, ups, vecs, cbf, cf)


def _rwkv_out_kernel(of_ref, ob_ref, bf_ref, bb_ref, xg_ref, gup_ref, lnw_ref, lnb_ref, bd_ref, o_ref):
    o = of_ref[...].astype(F32) + ob_ref[...].astype(F32)
    bd = bd_ref[...]
    inv_n = 1.0 / RWKV_HEAD
    tm = o.shape[0]

    def head_sum(x):
        s2 = _dot(jnp.concatenate(_split2(x), axis=0), bd)
        return s2[0:tm] + s2[tm:2 * tm]

    mu = head_sum(o) * inv_n
    oc = o - mu
    var = head_sum(oc * oc) * inv_n
    o_n = oc * lax.rsqrt(var + GN_EPS) * lnw_ref[...] + lnb_ref[...]
    gate = _dot(_sigmoid(xg_ref[...]).astype(BF16), gup_ref[...])
    bonus = bf_ref[...].astype(F32) + bb_ref[...].astype(F32)
    o_ref[...] = ((o_n + bonus) * gate).astype(o_ref.dtype)


def _rwkv_out(o_f, o_b, bon_f, bon_b, zs, gate_up, ln_w, ln_b, d):
    t = o_f.shape[0]
    gl = GROUP_LANES
    tm = _pick_tile(t, 1024, 8)
    xg_block = (3 * d + 4 * LANES) // gl
    bd = _wkv_constants()[0][2]
    big = pl.BlockSpec((tm, gl), lambda i, j: (i, j))
    return pl.pallas_call(
        _rwkv_out_kernel,
        out_shape=jax.ShapeDtypeStruct((t, d), BF16),
        grid=(t // tm, d // gl),
        in_specs=[
            big, big, big, big,
            pl.BlockSpec((tm, gl), lambda i, j: (i, xg_block)),
            pl.BlockSpec((gl, gl), lambda i, j: (0, j)),
            pl.BlockSpec((1, gl), lambda i, j: (0, j)),
            pl.BlockSpec((1, gl), lambda i, j: (0, j)),
            pl.BlockSpec((gl, gl), lambda i, j: (0, 0)),
        ],
        out_specs=big,
        compiler_params=_params("parallel", "parallel"),
        name="rwkv_out",
    )(o_f, o_b, bon_f, bon_b, zs, gate_up, ln_w, ln_b, bd)


def _attn_kernel(q_ref, k_ref, v_ref, cos_ref, sin_ref, lam_ref, sw_ref, o_ref, kr_ref, vt_ref, *,
                 tq, tk, lam_init):
    qi = pl.program_id(2)
    dh = DIFF_HEAD

    @pl.when(qi == 0)
    def _():
        blk = 512 if k_ref.shape[0] % 512 == 0 else k_ref.shape[0]
        for c0 in range(0, k_ref.shape[0], blk):
            cos = cos_ref[c0:c0 + blk, :]
            sin = sin_ref[c0:c0 + blk, :]
            for m in range(2):
                km = k_ref[c0:c0 + blk, m * dh:(m + 1) * dh].astype(F32)
                kr_ref[c0:c0 + blk, m * dh:(m + 1) * dh] = (
                    km * cos + pltpu.roll(km, dh // 2, 1) * sin).astype(BF16)
            vt_ref[:, c0:c0 + blk] = v_ref[c0:c0 + blk, :].astype(F32).T.astype(BF16)

    lam = (jnp.exp(jnp.sum(lam_ref[0:1, :] * lam_ref[1:2, :], axis=-1, keepdims=True))
           - jnp.exp(jnp.sum(lam_ref[2:3, :] * lam_ref[3:4, :], axis=-1, keepdims=True)) + lam_init)
    row0 = pl.multiple_of(qi * tq, tq)
    cq = cos_ref[pl.ds(row0, tq), :]
    sq = sin_ref[pl.ds(row0, tq), :]
    qs = []
    for m in range(2):
        qm = q_ref[:, m * dh:(m + 1) * dh].astype(F32)
        qm = (qm * cq + pltpu.roll(qm, dh // 2, 1) * sq) * (dh ** -0.5 * LOG2E)
        qs.append(qm.T.astype(BF16))
    nkv = k_ref.shape[0] // tk
    mx, ls, acc = [None, None], [None, None], [None, None]
    for j in range(nkv):
        vt = vt_ref[:, j * tk:(j + 1) * tk]
        for m in range(2):
            st = _dot(kr_ref[j * tk:(j + 1) * tk, m * dh:(m + 1) * dh], qs[m])
            tile_max = jnp.max(st, axis=0, keepdims=True)
            m_new = tile_max if j == 0 else jnp.maximum(mx[m], tile_max)
            p = jnp.exp2(st - m_new)
            psum = jnp.sum(p, axis=0, keepdims=True)
            pv = _dot(vt, p.astype(BF16))
            if j == 0:
                ls[m], acc[m] = psum, pv
            else:
                alpha = jnp.exp2(mx[m] - m_new)
                ls[m] = alpha * ls[m] + psum
                acc[m] = alpha * acc[m] + pv
            mx[m] = m_new
    o = (acc[0] * (1.0 / ls[0]) - acc[1] * (lam / ls[1])).T
    o = o * lax.rsqrt(jnp.mean(o * o, axis=-1, keepdims=True) + SUBLN_EPS) * sw_ref[...]
    o_ref[...] = (o * (1.0 - lam_init)).astype(o_ref.dtype)


def _diff_attention(z_att, cos, sin_signed, lam_vecs, subln_w, batch, seq, d, lam_init):
    t = z_att.shape[0]
    hw = 2 * DIFF_HEAD
    nh = d // hw
    tq = _pick_tile(seq, 256, 8)
    nq = seq // tq
    tk = _pick_tile(seq, MXU_DIM, LANES)
    kern = functools.partial(_attn_kernel, tq=tq, tk=tk, lam_init=lam_init)
    return pl.pallas_call(
        kern,
        out_shape=jax.ShapeDtypeStruct((t, d), BF16),
        grid=(batch, nh, nq),
        in_specs=[
            pl.BlockSpec((tq, hw), lambda b, h, i: (b * nq + i, h)),
            pl.BlockSpec((seq, hw), lambda b, h, i: (b, nh + h)),
            pl.BlockSpec((seq, hw), lambda b, h, i: (b, 2 * nh + h)),
            pl.BlockSpec((seq, DIFF_HEAD), lambda b, h, i: (0, 0)),
            pl.BlockSpec((seq, DIFF_HEAD), lambda b, h, i: (0, 0)),
            pl.BlockSpec((4, DIFF_HEAD), lambda b, h, i: (0, 0)),
            pl.BlockSpec((1, hw), lambda b, h, i: (0, 0)),
        ],
        out_specs=pl.BlockSpec((tq, hw), lambda b, h, i: (b * nq + i, h)),
        scratch_shapes=[pltpu.VMEM((seq, hw), BF16), pltpu.VMEM((hw, seq), BF16)],
        compiler_params=_params("parallel", "parallel", "arbitrary"),
        name="diff_attention",
    )(z_att, z_att, z_att, cos, sin_signed, lam_vecs, subln_w)


def _merge_kernel(ya_ref, yb_ref, pa_ref, pb_ref, ga_ref, gb_ref, o_ref):
    ma = _dot(ya_ref[...], pa_ref[...])
    mb = _dot(yb_ref[...], pb_ref[...])
    ga = _sigmoid(ga_ref[...].astype(F32))
    gb = _sigmoid(gb_ref[...].astype(F32))
    o_ref[...] = (ga * ma + gb * mb).astype(o_ref.dtype)


def _merge(y_a, y_b, proj_a, proj_b, z_att, d):
    t = y_a.shape[0]
    tm = _pick_tile(t, 1024, 8)
    tn = _pick_tile(d, 512, LANES)
    nb = d // tn
    return pl.pallas_call(
        _merge_kernel,
        out_shape=jax.ShapeDtypeStruct((t, d), BF16),
        grid=(t // tm, nb),
        in_specs=[
            pl.BlockSpec((tm, d), lambda i, j: (i, 0)),
            pl.BlockSpec((tm, d), lambda i, j: (i, 0)),
            pl.BlockSpec((d, tn), lambda i, j: (0, j)),
            pl.BlockSpec((d, tn), lambda i, j: (0, j)),
            pl.BlockSpec((tm, tn), lambda i, j: (i, 3 * nb + j)),
            pl.BlockSpec((tm, tn), lambda i, j: (i, 4 * nb + j)),
        ],
        out_specs=pl.BlockSpec((tm, tn), lambda i, j: (i, j)),
        compiler_params=_params("parallel", "arbitrary"),
        name="gated_merge",
    )(y_a, y_b, proj_a, proj_b, z_att, z_att)


def _mm_ln_kernel(a_ref, w_ref, x_ref, mod_ref, g_ref, b_ref, o_ref, *, nk, gate_row, alpha):
    kk = pl.program_id(1)
    tm = o_ref.shape[0]
    halves = [slice(0, tm // 2), slice(tm // 2, tm)]

    def layer_norm_rows(rows, acc):
        y = alpha * x_ref[rows, :] + mod_ref[gate_row:gate_row + 1, :] * acc
        mu = jnp.mean(y, axis=-1, keepdims=True)
        yc = y - mu
        var = jnp.mean(yc * yc, axis=-1, keepdims=True)
        o_ref[rows, :] = yc * lax.rsqrt(var + LN_EPS) * g_ref[...] + b_ref[...]

    def finish(accumulated):
        w = w_ref[...]
        acc0 = _dot(a_ref[halves[0], :], w)
        acc1 = _dot(a_ref[halves[1], :], w)
        if accumulated:
            acc0 = acc0 + o_ref[halves[0], :]
        layer_norm_rows(halves[0], acc0)
        if accumulated:
            acc1 = acc1 + o_ref[halves[1], :]
        layer_norm_rows(halves[1], acc1)

    if nk == 1:
        finish(False)
        return

    @pl.when(kk == 0)
    def _():
        o_ref[...] = _dot(a_ref[...], w_ref[...])

    @pl.when(jnp.logical_and(kk > 0, kk < nk - 1))
    def _():
        o_ref[...] += _dot(a_ref[...], w_ref[...])

    @pl.when(kk == nk - 1)
    def _():
        finish(True)


def _mm_residual_ln(a, w, x2, mod_l, gate_row, ln_g, ln_b, seq, alpha):
    t, kdim = a.shape
    d = w.shape[1]
    if kdim <= 2048:
        tm, tk = _pick_tile(seq, 512, 8), kdim
    else:
        tm, tk = _pick_tile(seq, 1024, 8), _pick_tile(kdim, 768, LANES)
    nk = kdim // tk
    tps = seq // tm
    kern = functools.partial(_mm_ln_kernel, nk=nk, gate_row=gate_row, alpha=alpha)
    return pl.pallas_call(
        kern,
        out_shape=jax.ShapeDtypeStruct((t, d), F32),
        grid=(t // tm, nk),
        in_specs=[
            pl.BlockSpec((tm, tk), lambda i, k: (i, k)),
            pl.BlockSpec((tk, d), lambda i, k: (k, 0)),
            pl.BlockSpec((tm, d), lambda i, k: (i, 0)),
            pl.BlockSpec((None, 6, d), lambda i, k: (i // tps, 0, 0)),
            pl.BlockSpec((1, d), lambda i, k: (0, 0)),
            pl.BlockSpec((1, d), lambda i, k: (0, 0)),
        ],
        out_specs=pl.BlockSpec((tm, d), lambda i, k: (i, 0)),
        compiler_params=_params("parallel", "arbitrary"),
        name="matmul_residual_ln",
    )(a, w, x2, mod_l, ln_g, ln_b)


def _ffn_up_kernel(x_ref, mod_ref, wg_ref, wu_ref, o_ref, u_ref, *, shift_row, scale_row):
    @pl.when(pl.program_id(1) == 0)
    def _():
        sc = 1.0 + mod_ref[scale_row:scale_row + 1, :]
        sh = mod_ref[shift_row:shift_row + 1, :]
        u_ref[...] = (x_ref[...] * sc + sh).astype(BF16)

    u = u_ref[...]
    hg = _dot(u, wg_ref[...])
    hu = _dot(u, wu_ref[...])
    o_ref[...] = (hg * _sigmoid(hg) * hu).astype(o_ref.dtype)


def _ffn_up(x2, mod_l, w_gate, w_up, seq):
    t, d = x2.shape
    f = w_gate.shape[1]
    tm = _pick_tile(seq, 1024, 8)
    tn = _pick_tile(f, 512, LANES)
    tps = seq // tm
    kern = functools.partial(_ffn_up_kernel, shift_row=3, scale_row=4)
    return pl.pallas_call(
        kern,
        out_shape=jax.ShapeDtypeStruct((t, f), BF16),
        grid=(t // tm, f // tn),
        in_specs=[
            pl.BlockSpec((tm, d), lambda i, j: (i, 0)),
            pl.BlockSpec((None, 6, d), lambda i, j: (i // tps, 0, 0)),
            pl.BlockSpec((d, tn), lambda i, j: (0, j)),
            pl.BlockSpec((d, tn), lambda i, j: (0, j)),
        ],
        out_specs=pl.BlockSpec((tm, tn), lambda i, j: (i, j)),
        scratch_shapes=[pltpu.VMEM((tm, d), BF16)],
        compiler_params=_params("parallel", "arbitrary"),
        name="ffn_up",
    )(x2, mod_l, w_gate, w_up)


def _pad_cols(a, width):
    return jnp.pad(a, [(0, 0)] * (a.ndim - 1) + [(0, width - a.shape[-1])])


def _pad_rows(a, height):
    return jnp.pad(a, [(0, 0)] * (a.ndim - 2) + [(0, height - a.shape[-2]), (0, 0)])


def _rope_tables(seq):
    pos = jnp.arange(seq, dtype=F32)
    inv = ROPE_THETA ** (-jnp.arange(0, DIFF_HEAD, 2, dtype=F32) / DIFF_HEAD)
    ang = pos[:, None] * inv[None, :]
    emb = jnp.concatenate([ang, ang], axis=-1)
    sign = jnp.where(jnp.arange(DIFF_HEAD) < DIFF_HEAD // 2, -1.0, 1.0).astype(F32)
    return jnp.cos(emb), jnp.sin(emb) * sign[None, :]


def kernel(x, c, ada_w, ada_b, w_in, shift_mu_prev, shift_mu_next, decay_w0, decay_up, iclr_a0, iclr_up, gate_up, k_k, k_a, r_k, ln_x_w, ln_x_b, vres_down, vres_up, vres_v0, lambda_q1, lambda_k1, lambda_q2, lambda_k2, subln_w, proj_a, proj_b, w_out, ln1_g, ln1_b, ffn_w_gate, ffn_w_up, ffn_w_down, ln2_g, ln2_b):
    batch, seq, d = x.shape
    depth = ada_w.shape[0]
    t = batch * seq
    dl, il, gl_rank = decay_up.shape[2], iclr_up.shape[2], gate_up.shape[1]
    vl = vres_down.shape[2]
    assert max(dl, il, vl) <= LANES and gl_rank <= GROUP_LANES
    rwkv_cols = 3 * d + 2 * dl + 2 * il + gl_rank
    alpha = (2 * depth) ** 0.25

    mod = _modulation(c, ada_w, ada_b)
    cos, sin_signed = _rope_tables(seq)
    x2 = x.reshape(t, d)
    zs_first = None

    for l in range(depth):
        mod_l = mod[l]
        w_l = w_in[l]
        gw = _groups_per_step(d) * GROUP_LANES
        cuts = np.cumsum([3 * d, dl, dl, il, il, gl_rank])

        def regroup(a):
            rkv, xw_f, xw_b, xa_f, xa_b, xg = jnp.split(a[..., :rwkv_cols], cuts[:-1], axis=-1)
            lead = rkv.shape[:-1]
            rkv = jnp.swapaxes(rkv.reshape(*lead, 3, d // gw, gw), -3, -2).reshape(*lead, 3 * d)
            return [rkv, _pad_cols(xw_f, LANES), _pad_cols(xa_f, LANES), _pad_cols(xw_b, LANES),
                    _pad_cols(xa_b, LANES), _pad_cols(xg, GROUP_LANES)]

        w_lb = w_l.astype(BF16)
        w_parts = regroup(w_lb)
        if l > 0:
            w_parts.append(_pad_cols(vres_down[l - 1].astype(BF16), LANES))
        else:
            w_parts.append(jnp.zeros((d, LANES), BF16))
        n_rwkv = sum(p.shape[1] for p in w_parts)
        n_pad = -(-n_rwkv // 1024) * 1024
        w_rwkv = _pad_cols(jnp.concatenate(w_parts, axis=1), n_pad)
        mu = jnp.stack([
            _pad_cols(jnp.concatenate(regroup(shift_mu_prev[l]), axis=0), n_pad),
            _pad_cols(jnp.concatenate(regroup(shift_mu_next[l]), axis=0), n_pad)])
        w_att = w_lb[:, rwkv_cols:]

        zs = _inproj_shift(x2, mod_l, w_rwkv, mu, seq, F32)
        z_att = _inproj_plain(x2, mod_l, w_att, seq, BF16)

        vup = _pad_rows(vres_up[l - 1], LANES) if l > 0 else jnp.zeros((LANES, d), F32)
        vv0 = vres_v0[l - 1] if l > 0 else jnp.zeros((d,), F32)
        ups = jnp.concatenate([_pad_rows(decay_up[l], LANES), _pad_rows(iclr_up[l], LANES), vup[None]],
                              axis=0).astype(BF16)
        vecs = jnp.stack([decay_w0[l, 0], decay_w0[l, 1], iclr_a0[l, 0], iclr_a0[l, 1],
                          k_k[l], k_a[l], r_k[l].reshape(d), vv0])
        o_f, o_b, bon_f, bon_b = _wkv(zs, zs_first if l > 0 else None, ups, vecs, batch, seq, d)
        if l == 0:
            zs_first = zs
        y_a = _rwkv_out(o_f, o_b, bon_f, bon_b, zs, _pad_rows(gate_up[l], GROUP_LANES).astype(BF16),
                        ln_x_w[l].reshape(1, d), ln_x_b[l].reshape(1, d), d)

        lam_init = 0.8 - 0.6 * math.exp(-0.3 * l)
        lam_vecs = jnp.stack([lambda_q1[l], lambda_k1[l], lambda_q2[l], lambda_k2[l]])
        y_b = _diff_attention(z_att, cos, sin_signed, lam_vecs, subln_w[l].reshape(1, -1),
                              batch, seq, d, lam_init)

        merged = _merge(y_a, y_b, proj_a[l].astype(BF16), proj_b[l].astype(BF16), z_att, d)
        x2 = _mm_residual_ln(merged, w_out[l].astype(BF16), x2, mod_l, 2,
                             ln1_g[l].reshape(1, d), ln1_b[l].reshape(1, d), seq, alpha)

        h = _ffn_up(x2, mod_l, ffn_w_gate[l].astype(BF16), ffn_w_up[l].astype(BF16), seq)
        x2 = _mm_residual_ln(h, ffn_w_down[l].astype(BF16), x2, mod_l, 5,
                             ln2_g[l].reshape(1, d), ln2_b[l].reshape(1, d), seq, alpha)

    return x2.reshape(batch, seq, d)
```

```python
import functools
import math

import numpy as np
import jax
import jax.numpy as jnp
from jax import lax
from jax.experimental import pallas as pl
from jax.experimental.pallas import tpu as pltpu

F32 = jnp.float32
BF16 = jnp.bfloat16
HIGHEST = lax.Precision.HIGHEST

RWKV_HEAD = 64
DIFF_HEAD = 128
ROPE_THETA = 10000.0
SUBLN_EPS = 1e-5
LN_EPS = 1e-5
GN_EPS = 1e-5 * RWKV_HEAD
EXP_NEG_HALF = math.exp(-0.5)
LOG2E = math.log2(math.e)

LANES = 128
MXU_DIM = 256
VMEM_LIMIT = 56 * 1024 * 1024

CHUNK = 64
HEADS_PER_GROUP = MXU_DIM // RWKV_HEAD
GROUP_LANES = HEADS_PER_GROUP * RWKV_HEAD
STACK_ROWS = HEADS_PER_GROUP * CHUNK
MAX_GROUPS_PER_STEP = 4
CHUNKS_PER_STEP = 2
STAGE_LAG = 4
STATE_BARRIER = "state-barrier"
CHAIN_DONE = "chain-done"
HALO = 16


def _sigmoid(x):
    return 1.0 / (1.0 + jnp.exp(-x))


def _dot(a, b, precision=None):
    return jnp.dot(a, b, preferred_element_type=F32, precision=precision)


def _dot_nt(a, b):
    return lax.dot_general(a, b, (((1,), (1,)), ((), ())), preferred_element_type=F32)


def _dot_tn(a, b):
    return lax.dot_general(a, b, (((0,), (0,)), ((), ())), preferred_element_type=F32)


def _split2(x):
    hi = x.astype(BF16)
    return hi, (x - hi.astype(F32)).astype(BF16)


def _params(*semantics):
    return pltpu.CompilerParams(dimension_semantics=semantics, vmem_limit_bytes=VMEM_LIMIT)


def _pick_tile(n, target, quantum):
    best = None
    t = quantum
    while t <= min(n, target):
        if n % t == 0:
            best = t
        t += quantum
    assert best is not None, (n, target, quantum)
    return best


def _mod_kernel(c_ref, w_ref, b_ref, o_ref):
    c = c_ref[...]
    c_act = c * _sigmoid(c)
    o_ref[...] = _dot(c_act, w_ref[...], HIGHEST) + b_ref[...]


def _modulation(c, ada_w, ada_b):
    nl, d, n6 = ada_w.shape
    b = c.shape[0]
    rows = -(-b // 8) * 8
    c_pad = jnp.pad(c, ((0, rows - b), (0, 0)))
    tn = _pick_tile(n6, 2048, LANES)
    out = pl.pallas_call(
        _mod_kernel,
        out_shape=jax.ShapeDtypeStruct((nl, rows, n6), F32),
        grid=(nl, n6 // tn),
        in_specs=[
            pl.BlockSpec((rows, d), lambda l, j: (0, 0)),
            pl.BlockSpec((None, d, tn), lambda l, j: (l, 0, j)),
            pl.BlockSpec((None, 1, tn), lambda l, j: (l, 0, j)),
        ],
        out_specs=pl.BlockSpec((None, rows, tn), lambda l, j: (l, 0, j)),
        compiler_params=_params("parallel", "parallel"),
        name="adaln_mod",
    )(c_pad, ada_w, ada_b.reshape(nl, 1, n6))
    return out[:, :b].reshape(nl, b, 6, d)


def _inproj_kernel(x_ref, xp_ref, xn_ref, mod_ref, w_ref, mu_ref, o_ref, u_ref, *acc_refs,
                   tm, tiles_per_seq, shift_row, scale_row):
    i = pl.program_id(0)
    j = pl.program_id(1)

    @pl.when(j == 0)
    def _():
        sc = 1.0 + mod_ref[scale_row:scale_row + 1, :]
        sh = mod_ref[shift_row:shift_row + 1, :]
        u_ref[0:HALO, :] = (xp_ref[...] * sc + sh).astype(BF16)
        u_ref[HALO:HALO + tm, :] = (x_ref[...] * sc + sh).astype(BF16)
        u_ref[HALO + tm:, :] = (xn_ref[...] * sc + sh).astype(BF16)

    row = lax.broadcasted_iota(jnp.int32, (tm, 1), 0)
    pos = i % tiles_per_seq
    first = jnp.logical_and(row == 0, pos == 0)
    last = jnp.logical_and(row == tm - 1, pos == tiles_per_seq - 1)

    def shift_out(acc_ref, cols):
        zc = acc_ref[HALO:HALO + tm, :]
        zp = jnp.where(first, 0.0, acc_ref[HALO - 1:HALO - 1 + tm, :])
        zn = jnp.where(last, 0.0, acc_ref[HALO + 1:HALO + 1 + tm, :])
        o_ref[:, cols] = (zc + mu_ref[0:1, cols] * (zp - zc) + mu_ref[1:2, cols] * (zn - zc)).astype(o_ref.dtype)

    sub = acc_refs[0].shape[1]
    parts = [slice(p * sub, (p + 1) * sub) for p in range(len(acc_refs))]
    for p, cols in enumerate(parts):
        acc_refs[p][...] = _dot(u_ref[...], w_ref[:, cols])
        if p > 0:
            shift_out(acc_refs[p - 1], parts[p - 1])
    shift_out(acc_refs[-1], parts[-1])


def _plain_inproj_kernel(x_ref, mod_ref, w_ref, o_ref, u_ref, *, shift_row, scale_row):
    @pl.when(pl.program_id(1) == 0)
    def _():
        sc = 1.0 + mod_ref[scale_row:scale_row + 1, :]
        sh = mod_ref[shift_row:shift_row + 1, :]
        u_ref[...] = (x_ref[...] * sc + sh).astype(BF16)

    o_ref[...] = _dot(u_ref[...], w_ref[...]).astype(o_ref.dtype)


def _inproj_shift(x2, mod_l, w, mu, seq, out_dtype):
    t, d = x2.shape
    n = w.shape[1]
    tm = _pick_tile(seq, 1024, HALO)
    tn = _pick_tile(n, 1024, 2 * MXU_DIM)
    sub = 2 * MXU_DIM
    tps = seq // tm
    hb = tm // HALO
    last_halo = t // HALO - 1
    kern = functools.partial(_inproj_kernel, tm=tm, tiles_per_seq=tps, shift_row=0, scale_row=1)
    return pl.pallas_call(
        kern,
        out_shape=jax.ShapeDtypeStruct((t, n), out_dtype),
        grid=(t // tm, n // tn),
        in_specs=[
            pl.BlockSpec((tm, d), lambda i, j: (i, 0)),
            pl.BlockSpec((HALO, d), lambda i, j: (jnp.maximum(i * hb - 1, 0), 0)),
            pl.BlockSpec((HALO, d), lambda i, j: (jnp.minimum((i + 1) * hb, last_halo), 0)),
            pl.BlockSpec((None, 6, d), lambda i, j: (i // tps, 0, 0)),
            pl.BlockSpec((d, tn), lambda i, j: (0, j)),
            pl.BlockSpec((2, tn), lambda i, j: (0, j)),
        ],
        out_specs=pl.BlockSpec((tm, tn), lambda i, j: (i, j)),
        scratch_shapes=[pltpu.VMEM((tm + 2 * HALO, d), BF16)]
        + [pltpu.VMEM((tm + 2 * HALO, sub), F32) for _ in range(tn // sub)],
        compiler_params=_params("parallel", "arbitrary"),
        name="inproj_shift",
    )(x2, x2, x2, mod_l, w, mu)


def _inproj_plain(x2, mod_l, w, seq, out_dtype):
    t, d = x2.shape
    n = w.shape[1]
    tm = _pick_tile(seq, 1024, HALO)
    tn = _pick_tile(n, 1024, LANES)
    tps = seq // tm
    kern = functools.partial(_plain_inproj_kernel, shift_row=0, scale_row=1)
    return pl.pallas_call(
        kern,
        out_shape=jax.ShapeDtypeStruct((t, n), out_dtype),
        grid=(t // tm, n // tn),
        in_specs=[
            pl.BlockSpec((tm, d), lambda i, j: (i, 0)),
            pl.BlockSpec((None, 6, d), lambda i, j: (i // tps, 0, 0)),
            pl.BlockSpec((d, tn), lambda i, j: (0, j)),
        ],
        out_specs=pl.BlockSpec((tm, tn), lambda i, j: (i, j)),
        scratch_shapes=[pltpu.VMEM((tm, d), BF16)],
        compiler_params=_params("parallel", "arbitrary"),
        name="inproj_plain",
    )(x2, mod_l, w)


def _wkv_constants():
    c, gc, gl = CHUNK, STACK_ROWS, GROUP_LANES
    assert gc == gl
    t = np.arange(c)
    lower = t[None, :] <= t[:, None]
    col_t = np.arange(gc) % c
    row_h = np.arange(gc) // c
    lane_h = np.arange(gl) // RWKV_HEAD
    tri = np.zeros((gc, gc), bool)
    tri[0:c, 0:c] = lower
    tri[c:2 * c, 0:c] = lower.T
    bd_lanes = lane_h[:, None] == lane_h[None, :]
    slab_h = np.stack([row_h[:, None] == lane_h[None, :], row_h[:, None] == row_h[None, :], bd_lanes, tri])
    slab_f = np.concatenate([
        col_t[None, :] < t[:, None], col_t[None, :] > t[:, None],
        col_t[None, :] <= t[:, None], col_t[None, :] >= t[:, None],
        col_t[None, :] == t[:, None], bd_lanes])
    return jnp.asarray(slab_h, BF16), jnp.asarray(slab_f, F32)


def _groups_per_step(d):
    groups = d // GROUP_LANES
    return max(g for g in range(1, MAX_GROUPS_PER_STEP + 1) if groups % g == 0)


def _wkv_chain(d, gi, gps, rows, zrkv, th, xab, vrb, v0, ups, vecs, cbf, cf, q_ref, o_ref, bon_ref,
               has_vres, state_ready):
    c, g, gc, gl = CHUNK, HEADS_PER_GROUP, STACK_ROWS, GROUP_LANES
    w = gps * gl
    sl = slice(gi * gl, (gi + 1) * gl)
    r = zrkv[rows, gi * gl:(gi + 1) * gl]
    k = zrkv[rows, w + gi * gl:w + (gi + 1) * gl]
    v = zrkv[rows, 2 * w + gi * gl:2 * w + (gi + 1) * gl]
    w0, a0 = vecs[d:d + 1, sl], vecs[2 + d:3 + d, sl]
    k_k, k_a, r_k = vecs[4:5, sl], vecs[5:6, sl], vecs[6:7, sl]
    sm, bdm, bdl = cbf[0], cbf[1], cbf[2]
    tri = cbf[3, d * c:(d + 1) * c, 0:c]
    ms = cf[d * c:(d + 1) * c, :] > 0.5
    mi = cf[(2 + d) * c:(3 + d) * c, :] > 0.5
    eye_w = cf[4 * c:5 * c, :]

    wl = _dot(th, ups[d, :, sl])
    al = _dot(xab, ups[2 + d, :, sl])
    if has_vres:
        ml = _dot(vrb, ups[4, :, sl])
    yield
    if has_vres:
        v = v + (v0[rows, sl] - v) * _sigmoid(vecs[7:8, sl] + ml)
    lw = -EXP_NEG_HALF * _sigmoid(w0 + wl)
    a = _sigmoid(a0 + al)
    kd = k * (1.0 + (a - 1.0) * k_a)
    kk0 = k * k_k
    sums = _dot(jnp.concatenate([*_split2(kk0 * kk0), *_split2(r * kd * r_k)], axis=0), bdl)
    cum2 = _dot(tri, jnp.concatenate(_split2(lw), axis=1))
    yield
    kk = kk0 * lax.rsqrt(jnp.maximum(sums[0:c] + sums[c:2 * c], 1e-24))
    bon_ref[rows, sl] = ((sums[2 * c:3 * c] + sums[3 * c:4 * c]) * v).astype(bon_ref.dtype)
    cum = cum2[:, 0:gl] + cum2[:, gl:2 * gl]
    tot = jnp.sum(lw, axis=0, keepdims=True)
    g_inv = jnp.exp(-cum)
    b = kk * a
    a_t = -(kk * jnp.exp(cum - lw))
    b_t = b * g_inv
    k_t = kd * g_inv
    r_t = r * jnp.exp(cum)
    g_rem = jnp.exp(tot - cum)
    bk_g = jnp.concatenate([b * g_rem, kd * g_rem], axis=0).astype(BF16)

    def stack(x):
        return jnp.concatenate([x.astype(BF16)] * g, axis=0) * sm

    def to_bd(xw):
        return jnp.concatenate([xw.astype(BF16)] * g, axis=0) * bdm

    ar = jnp.concatenate([a_t, r_t], axis=0).astype(BF16)
    nn = _dot_nt(ar, jnp.concatenate([stack(b_t), stack(k_t)], axis=0))
    if state_ready:
        q = q_ref[...]
        arq = _dot_nt(ar, q.astype(BF16))
    yield
    n_ab = jnp.where(ms, nn[0:c, 0:gc], 0.0)
    n_ak = jnp.where(ms, nn[0:c, gc:2 * gc], 0.0)
    n_rb = jnp.where(mi, nn[c:2 * c, 0:gc], 0.0).astype(BF16)
    n_rk = jnp.where(mi, nn[c:2 * c, gc:2 * gc], 0.0)
    pw = _dot(n_ab.astype(BF16), to_bd(n_ab))
    nv = _dot(jnp.concatenate([n_ak.astype(BF16), n_rk.astype(BF16)], axis=0), stack(v))
    yield
    inv = eye_w + n_ab
    steps = int(math.log2(c)) - 1
    for s in range(steps):
        rhs = to_bd(pw)
        if s < steps - 1:
            both = _dot(jnp.concatenate([pw.astype(BF16), inv.astype(BF16)], axis=0), rhs)
            yield
            pw = both[0:c]
            inv = inv + both[c:2 * c]
        else:
            last = _dot(inv.astype(BF16), rhs)
            yield
            inv = inv + last
    if not state_ready:
        yield STATE_BARRIER
        q = q_ref[...]
        arq = _dot_nt(ar, q.astype(BF16))
        yield
    y = arq[0:c] + nv[0:c]
    u = _dot(inv.astype(BF16), stack(y))
    yield
    o_u = _dot(n_rb, stack(u))
    upd = _dot_tn(jnp.concatenate([u, v], axis=0).astype(BF16), bk_g)
    yield
    o_ref[rows, sl] = (arq[c:2 * c] + nv[c:2 * c] + o_u).astype(o_ref.dtype)
    q_ref[...] = (q * jnp.exp(tot) + upd) * cf[5 * c:5 * c + gl, :]


def _wkv_kernel(*refs, has_vres, gps):
    n_act = 4 if has_vres else 2
    acts = [list(refs[0:n_act]), list(refs[n_act:2 * n_act])]
    ups, vecs, cbf, cf = refs[2 * n_act:2 * n_act + 4]
    outs = refs[2 * n_act + 4:2 * n_act + 8]
    q_refs = refs[2 * n_act + 8]

    @pl.when(pl.program_id(2) == 0)
    def _():
        q_refs[...] = jnp.zeros_like(q_refs)

    waves = []
    for sub in range(CHUNKS_PER_STEP):
        wave = []
        for d in range(2):
            pos = CHUNKS_PER_STEP - 1 - sub if d == 1 else sub
            rows = slice(pos * CHUNK, (pos + 1) * CHUNK)
            zrkv, xwa = acts[d][0:2]
            v0, vr = (acts[d][2], acts[d][3]) if has_vres else (None, None)
            th = jnp.tanh(xwa[rows, 0:LANES]).astype(BF16)
            xab = xwa[rows, LANES:2 * LANES].astype(BF16)
            vrb = vr[rows, :].astype(BF16) if has_vres else None
            for gi in range(gps):
                wave.append(_wkv_chain(d, gi, gps, rows, zrkv, th, xab, vrb, v0, ups, vecs, cbf, cf,
                                       q_refs.at[d * gps + gi], outs[d], outs[2 + d], has_vres, sub == 0))
        waves.append(wave)
    done = [[False] * len(wave) for wave in waves]
    held = [[False] * len(wave) for wave in waves]
    rounds = 0
    while not all(all(flags) for flags in done):
        for wi, wave in enumerate(waves):
            if rounds < wi * STAGE_LAG:
                continue
            for ci, chain in enumerate(wave):
                if done[wi][ci] or (held[wi][ci] and not done[wi - 1][ci]):
                    continue
                held[wi][ci] = False
                signal = next(chain, CHAIN_DONE)
                if signal is CHAIN_DONE:
                    done[wi][ci] = True
                elif signal is STATE_BARRIER:
                    held[wi][ci] = True
        rounds += 1


def _wkv(zs, zs_first, ups, vecs, batch, seq, d):
    t = zs.shape[0]
    has_vres = zs_first is not None
    c, gl = CHUNKS_PER_STEP * CHUNK, GROUP_LANES
    gps = _groups_per_step(d)
    w = gps * gl
    assert seq % c == 0 and d % w == 0
    nc = seq // c
    ng = d // w
    lora_pair0 = 3 * d // (2 * LANES)
    vres_block = (3 * d + 4 * LANES + gl) // LANES

    def dir_specs(rev):
        rw = (lambda b, i: b * nc + (nc - 1 - i)) if rev else (lambda b, i: b * nc + i)
        specs = [
            pl.BlockSpec((c, 3 * w), lambda b, g, i: (rw(b, i), g)),
            pl.BlockSpec((c, 2 * LANES), lambda b, g, i: (rw(b, i), lora_pair0 + rev)),
        ]
        args = [zs, zs]
        if has_vres:
            specs += [
                pl.BlockSpec((c, w), lambda b, g, i: (rw(b, i), 3 * g + 2)),
                pl.BlockSpec((c, LANES), lambda b, g, i: (rw(b, i), vres_block)),
            ]
            args += [zs_first, zs]
        return specs, args

    fs, fa = dir_specs(0)
    bs, ba = dir_specs(1)
    cbf, cf = _wkv_constants()
    p_specs = [
        pl.BlockSpec((5, LANES, w), lambda b, g, i: (0, 0, g)),
        pl.BlockSpec((8, w), lambda b, g, i: (0, g)),
        pl.BlockSpec(cbf.shape, lambda b, g, i: (0, 0, 0)),
        pl.BlockSpec(cf.shape, lambda b, g, i: (0, 0)),
    ]
    out_f = pl.BlockSpec((c, w), lambda b, g, i: (b * nc + i, g))
    out_b = pl.BlockSpec((c, w), lambda b, g, i: (b * nc + (nc - 1 - i), g))
    shp = jax.ShapeDtypeStruct((t, d), BF16)
    return pl.pallas_call(
        functools.partial(_wkv_kernel, has_vres=has_vres, gps=gps),
        out_shape=[shp, shp, shp, shp],
        grid=(batch, ng, nc),
        in_specs=fs + bs + p_specs,
        out_specs=[out_f, out_b, out_f, out_b],
        scratch_shapes=[pltpu.VMEM((2 * gps, gl, gl), F32)],
        compiler_params=_params("parallel", "parallel", "arbitrary"),
        name="wkv7_chunked",
    )(*fa, *ba, ups, vecs, cbf, cf)


def _rwkv_out_kernel(of_ref, ob_ref, bf_ref, bb_ref, xg_ref, gup_ref, lnw_ref, lnb_ref, bd_ref, o_ref):
    o = of_ref[...].astype(F32) + ob_ref[...].astype(F32)
    bd = bd_ref[...]
    inv_n = 1.0 / RWKV_HEAD
    tm = o.shape[0]

    def head_sum(x):
        s2 = _dot(jnp.concatenate(_split2(x), axis=0), bd)
        return s2[0:tm] + s2[tm:2 * tm]

    mu = head_sum(o) * inv_n
    oc = o - mu
    var = head_sum(oc * oc) * inv_n
    o_n = oc * lax.rsqrt(var + GN_EPS) * lnw_ref[...] + lnb_ref[...]
    gate = _dot(_sigmoid(xg_ref[...]).astype(BF16), gup_ref[...])
    bonus = bf_ref[...].astype(F32) + bb_ref[...].astype(F32)
    o_ref[...] = ((o_n + bonus) * gate).astype(o_ref.dtype)


def _rwkv_out(o_f, o_b, bon_f, bon_b, zs, gate_up, ln_w, ln_b, d):
    t = o_f.shape[0]
    gl = GROUP_LANES
    tm = _pick_tile(t, 1024, 8)
    xg_block = (3 * d + 4 * LANES) // gl
    bd = _wkv_constants()[0][2]
    big = pl.BlockSpec((tm, gl), lambda i, j: (i, j))
    return pl.pallas_call(
        _rwkv_out_kernel,
        out_shape=jax.ShapeDtypeStruct((t, d), BF16),
        grid=(t // tm, d // gl),
        in_specs=[
            big, big, big, big,
            pl.BlockSpec((tm, gl), lambda i, j: (i, xg_block)),
            pl.BlockSpec((gl, gl), lambda i, j: (0, j)),
            pl.BlockSpec((1, gl), lambda i, j: (0, j)),
            pl.BlockSpec((1, gl), lambda i, j: (0, j)),
            pl.BlockSpec((gl, gl), lambda i, j: (0, 0)),
        ],
        out_specs=big,
        compiler_params=_params("parallel", "parallel"),
        name="rwkv_out",
    )(o_f, o_b, bon_f, bon_b, zs, gate_up, ln_w, ln_b, bd)


def _attn_kernel(q_ref, k_ref, v_ref, cos_ref, sin_ref, lam_ref, sw_ref, o_ref, kr_ref, vt_ref, *,
                 tq, tk, lam_init):
    qi = pl.program_id(2)
    dh = DIFF_HEAD

    @pl.when(qi == 0)
    def _():
        blk = 512 if k_ref.shape[0] % 512 == 0 else k_ref.shape[0]
        for c0 in range(0, k_ref.shape[0], blk):
            cos = cos_ref[c0:c0 + blk, :]
            sin = sin_ref[c0:c0 + blk, :]
            for m in range(2):
                km = k_ref[c0:c0 + blk, m * dh:(m + 1) * dh].astype(F32)
                kr_ref[c0:c0 + blk, m * dh:(m + 1) * dh] = (
                    km * cos + pltpu.roll(km, dh // 2, 1) * sin).astype(BF16)
            vt_ref[:, c0:c0 + blk] = v_ref[c0:c0 + blk, :].astype(F32).T.astype(BF16)

    lam = (jnp.exp(jnp.sum(lam_ref[0:1, :] * lam_ref[1:2, :], axis=-1, keepdims=True))
           - jnp.exp(jnp.sum(lam_ref[2:3, :] * lam_ref[3:4, :], axis=-1, keepdims=True)) + lam_init)
    row0 = pl.multiple_of(qi * tq, tq)
    cq = cos_ref[pl.ds(row0, tq), :]
    sq = sin_ref[pl.ds(row0, tq), :]
    qs = []
    for m in range(2):
        qm = q_ref[:, m * dh:(m + 1) * dh].astype(F32)
        qm = (qm * cq + pltpu.roll(qm, dh // 2, 1) * sq) * (dh ** -0.5 * LOG2E)
        qs.append(qm.T.astype(BF16))
    nkv = k_ref.shape[0] // tk
    mx, ls, acc = [None, None], [None, None], [None, None]
    for j in range(nkv):
        vt = vt_ref[:, j * tk:(j + 1) * tk]
        for m in range(2):
            st = _dot(kr_ref[j * tk:(j + 1) * tk, m * dh:(m + 1) * dh], qs[m])
            tile_max = jnp.max(st, axis=0, keepdims=True)
            m_new = tile_max if j == 0 else jnp.maximum(mx[m], tile_max)
            p = jnp.exp2(st - m_new)
            psum = jnp.sum(p, axis=0, keepdims=True)
            pv = _dot(vt, p.astype(BF16))
            if j == 0:
                ls[m], acc[m] = psum, pv
            else:
                alpha = jnp.exp2(mx[m] - m_new)
                ls[m] = alpha * ls[m] + psum
                acc[m] = alpha * acc[m] + pv
            mx[m] = m_new
    o = (acc[0] * (1.0 / ls[0]) - acc[1] * (lam / ls[1])).T
    o = o * lax.rsqrt(jnp.mean(o * o, axis=-1, keepdims=True) + SUBLN_EPS) * sw_ref[...]
    o_ref[...] = (o * (1.0 - lam_init)).astype(o_ref.dtype)


def _diff_attention(z_att, cos, sin_signed, lam_vecs, subln_w, batch, seq, d, lam_init):
    t = z_att.shape[0]
    hw = 2 * DIFF_HEAD
    nh = d // hw
    tq = _pick_tile(seq, 256, 8)
    nq = seq // tq
    tk = _pick_tile(seq, MXU_DIM, LANES)
    kern = functools.partial(_attn_kernel, tq=tq, tk=tk, lam_init=lam_init)
    return pl.pallas_call(
        kern,
        out_shape=jax.ShapeDtypeStruct((t, d), BF16),
        grid=(batch, nh, nq),
        in_specs=[
            pl.BlockSpec((tq, hw), lambda b, h, i: (b * nq + i, h)),
            pl.BlockSpec((seq, hw), lambda b, h, i: (b, nh + h)),
            pl.BlockSpec((seq, hw), lambda b, h, i: (b, 2 * nh + h)),
            pl.BlockSpec((seq, DIFF_HEAD), lambda b, h, i: (0, 0)),
            pl.BlockSpec((seq, DIFF_HEAD), lambda b, h, i: (0, 0)),
            pl.BlockSpec((4, DIFF_HEAD), lambda b, h, i: (0, 0)),
            pl.BlockSpec((1, hw), lambda b, h, i: (0, 0)),
        ],
        out_specs=pl.BlockSpec((tq, hw), lambda b, h, i: (b * nq + i, h)),
        scratch_shapes=[pltpu.VMEM((seq, hw), BF16), pltpu.VMEM((hw, seq), BF16)],
        compiler_params=_params("parallel", "parallel", "arbitrary"),
        name="diff_attention",
    )(z_att, z_att, z_att, cos, sin_signed, lam_vecs, subln_w)


def _merge_kernel(ya_ref, yb_ref, pa_ref, pb_ref, ga_ref, gb_ref, o_ref):
    ma = _dot(ya_ref[...], pa_ref[...])
    mb = _dot(yb_ref[...], pb_ref[...])
    ga = _sigmoid(ga_ref[...].astype(F32))
    gb = _sigmoid(gb_ref[...].astype(F32))
    o_ref[...] = (ga * ma + gb * mb).astype(o_ref.dtype)


def _merge(y_a, y_b, proj_a, proj_b, z_att, d):
    t = y_a.shape[0]
    tm = _pick_tile(t, 1024, 8)
    tn = _pick_tile(d, 512, LANES)
    nb = d // tn
    return pl.pallas_call(
        _merge_kernel,
        out_shape=jax.ShapeDtypeStruct((t, d), BF16),
        grid=(t // tm, nb),
        in_specs=[
            pl.BlockSpec((tm, d), lambda i, j: (i, 0)),
            pl.BlockSpec((tm, d), lambda i, j: (i, 0)),
            pl.BlockSpec((d, tn), lambda i, j: (0, j)),
            pl.BlockSpec((d, tn), lambda i, j: (0, j)),
            pl.BlockSpec((tm, tn), lambda i, j: (i, 3 * nb + j)),
            pl.BlockSpec((tm, tn), lambda i, j: (i, 4 * nb + j)),
        ],
        out_specs=pl.BlockSpec((tm, tn), lambda i, j: (i, j)),
        compiler_params=_params("parallel", "arbitrary"),
        name="gated_merge",
    )(y_a, y_b, proj_a, proj_b, z_att, z_att)


def _mm_ln_kernel(a_ref, w_ref, x_ref, mod_ref, g_ref, b_ref, o_ref, *, nk, gate_row, alpha):
    kk = pl.program_id(1)
    tm = o_ref.shape[0]
    halves = [slice(0, tm // 2), slice(tm // 2, tm)]

    def layer_norm_rows(rows, acc):
        y = alpha * x_ref[rows, :] + mod_ref[gate_row:gate_row + 1, :] * acc
        mu = jnp.mean(y, axis=-1, keepdims=True)
        yc = y - mu
        var = jnp.mean(yc * yc, axis=-1, keepdims=True)
        o_ref[rows, :] = yc * lax.rsqrt(var + LN_EPS) * g_ref[...] + b_ref[...]

    def finish(accumulated):
        w = w_ref[...]
        acc0 = _dot(a_ref[halves[0], :], w)
        acc1 = _dot(a_ref[halves[1], :], w)
        if accumulated:
            acc0 = acc0 + o_ref[halves[0], :]
        layer_norm_rows(halves[0], acc0)
        if accumulated:
            acc1 = acc1 + o_ref[halves[1], :]
        layer_norm_rows(halves[1], acc1)

    if nk == 1:
        finish(False)
        return

    @pl.when(kk == 0)
    def _():
        o_ref[...] = _dot(a_ref[...], w_ref[...])

    @pl.when(jnp.logical_and(kk > 0, kk < nk - 1))
    def _():
        o_ref[...] += _dot(a_ref[...], w_ref[...])

    @pl.when(kk == nk - 1)
    def _():
        finish(True)


def _mm_residual_ln(a, w, x2, mod_l, gate_row, ln_g, ln_b, seq, alpha):
    t, kdim = a.shape
    d = w.shape[1]
    if kdim <= 2048:
        tm, tk = _pick_tile(seq, 512, 8), kdim
    else:
        tm, tk = _pick_tile(seq, 1024, 8), _pick_tile(kdim, 768, LANES)
    nk = kdim // tk
    tps = seq // tm
    kern = functools.partial(_mm_ln_kernel, nk=nk, gate_row=gate_row, alpha=alpha)
    return pl.pallas_call(
        kern,
        out_shape=jax.ShapeDtypeStruct((t, d), F32),
        grid=(t // tm, nk),
        in_specs=[
            pl.BlockSpec((tm, tk), lambda i, k: (i, k)),
            pl.BlockSpec((tk, d), lambda i, k: (k, 0)),
            pl.BlockSpec((tm, d), lambda i, k: (i, 0)),
            pl.BlockSpec((None, 6, d), lambda i, k: (i // tps, 0, 0)),
            pl.BlockSpec((1, d), lambda i, k: (0, 0)),
            pl.BlockSpec((1, d), lambda i, k: (0, 0)),
        ],
        out_specs=pl.BlockSpec((tm, d), lambda i, k: (i, 0)),
        compiler_params=_params("parallel", "arbitrary"),
        name="matmul_residual_ln",
    )(a, w, x2, mod_l, ln_g, ln_b)


def _ffn_up_kernel(x_ref, mod_ref, wg_ref, wu_ref, o_ref, u_ref, *, shift_row, scale_row):
    @pl.when(pl.program_id(1) == 0)
    def _():
        sc = 1.0 + mod_ref[scale_row:scale_row + 1, :]
        sh = mod_ref[shift_row:shift_row + 1, :]
        u_ref[...] = (x_ref[...] * sc + sh).astype(BF16)

    u = u_ref[...]
    hg = _dot(u, wg_ref[...])
    hu = _dot(u, wu_ref[...])
    o_ref[...] = (hg * _sigmoid(hg) * hu).astype(o_ref.dtype)


def _ffn_up(x2, mod_l, w_gate, w_up, seq):
    t, d = x2.shape
    f = w_gate.shape[1]
    tm = _pick_tile(seq, 1024, 8)
    tn = _pick_tile(f, 512, LANES)
    tps = seq // tm
    kern = functools.partial(_ffn_up_kernel, shift_row=3, scale_row=4)
    return pl.pallas_call(
        kern,
        out_shape=jax.ShapeDtypeStruct((t, f), BF16),
        grid=(t // tm, f // tn),
        in_specs=[
            pl.BlockSpec((tm, d), lambda i, j: (i, 0)),
            pl.BlockSpec((None, 6, d), lambda i, j: (i // tps, 0, 0)),
            pl.BlockSpec((d, tn), lambda i, j: (0, j)),
            pl.BlockSpec((d, tn), lambda i, j: (0, j)),
        ],
        out_specs=pl.BlockSpec((tm, tn), lambda i, j: (i, j)),
        scratch_shapes=[pltpu.VMEM((tm, d), BF16)],
        compiler_params=_params("parallel", "arbitrary"),
        name="ffn_up",
    )(x2, mod_l, w_gate, w_up)


def _pad_cols(a, width):
    return jnp.pad(a, [(0, 0)] * (a.ndim - 1) + [(0, width - a.shape[-1])])


def _pad_rows(a, height):
    return jnp.pad(a, [(0, 0)] * (a.ndim - 2) + [(0, height - a.shape[-2]), (0, 0)])


def _rope_tables(seq):
    pos = jnp.arange(seq, dtype=F32)
    inv = ROPE_THETA ** (-jnp.arange(0, DIFF_HEAD, 2, dtype=F32) / DIFF_HEAD)
    ang = pos[:, None] * inv[None, :]
    emb = jnp.concatenate([ang, ang], axis=-1)
    sign = jnp.where(jnp.arange(DIFF_HEAD) < DIFF_HEAD // 2, -1.0, 1.0).astype(F32)
    return jnp.cos(emb), jnp.sin(emb) * sign[None, :]


def kernel(x, c, ada_w, ada_b, w_in, shift_mu_prev, shift_mu_next, decay_w0, decay_up, iclr_a0, iclr_up, gate_up, k_k, k_a, r_k, ln_x_w, ln_x_b, vres_down, vres_up, vres_v0, lambda_q1, lambda_k1, lambda_q2, lambda_k2, subln_w, proj_a, proj_b, w_out, ln1_g, ln1_b, ffn_w_gate, ffn_w_up, ffn_w_down, ln2_g, ln2_b):
    batch, seq, d = x.shape
    depth = ada_w.shape[0]
    t = batch * seq
    dl, il, gl_rank = decay_up.shape[2], iclr_up.shape[2], gate_up.shape[1]
    vl = vres_down.shape[2]
    assert max(dl, il, vl) <= LANES and gl_rank <= GROUP_LANES
    rwkv_cols = 3 * d + 2 * dl + 2 * il + gl_rank
    alpha = (2 * depth) ** 0.25

    mod = _modulation(c, ada_w, ada_b)
    cos, sin_signed = _rope_tables(seq)
    x2 = x.reshape(t, d)
    zs_first = None

    for l in range(depth):
        mod_l = mod[l]
        w_l = w_in[l]
        gw = _groups_per_step(d) * GROUP_LANES
        cuts = np.cumsum([3 * d, dl, dl, il, il, gl_rank])

        def regroup(a):
            rkv, xw_f, xw_b, xa_f, xa_b, xg = jnp.split(a[..., :rwkv_cols], cuts[:-1], axis=-1)
            blocks = [rkv[..., p * d + g * gw:p * d + (g + 1) * gw] for g in range(d // gw) for p in range(3)]
            return blocks + [_pad_cols(xw_f, LANES), _pad_cols(xa_f, LANES), _pad_cols(xw_b, LANES),
                             _pad_cols(xa_b, LANES), _pad_cols(xg, GROUP_LANES)]

        w_lb = w_l.astype(BF16)
        w_parts = regroup(w_lb)
        if l > 0:
            w_parts.append(_pad_cols(vres_down[l - 1].astype(BF16), LANES))
        else:
            w_parts.append(jnp.zeros((d, LANES), BF16))
        n_rwkv = sum(p.shape[1] for p in w_parts)
        n_pad = -(-n_rwkv // 1024) * 1024
        w_rwkv = _pad_cols(jnp.concatenate(w_parts, axis=1), n_pad)
        mu = jnp.stack([
            _pad_cols(jnp.concatenate(regroup(shift_mu_prev[l]), axis=0), n_pad),
            _pad_cols(jnp.concatenate(regroup(shift_mu_next[l]), axis=0), n_pad)])
        w_att = w_lb[:, rwkv_cols:]

        zs = _inproj_shift(x2, mod_l, w_rwkv, mu, seq, F32)
        z_att = _inproj_plain(x2, mod_l, w_att, seq, BF16)

        vup = _pad_rows(vres_up[l - 1], LANES) if l > 0 else jnp.zeros((LANES, d), F32)
        vv0 = vres_v0[l - 1] if l > 0 else jnp.zeros((d,), F32)
        ups = jnp.concatenate([_pad_rows(decay_up[l], LANES), _pad_rows(iclr_up[l], LANES), vup[None]],
                              axis=0).astype(BF16)
        vecs = jnp.stack([decay_w0[l, 0], decay_w0[l, 1], iclr_a0[l, 0], iclr_a0[l, 1],
                          k_k[l], k_a[l], r_k[l].reshape(d), vv0])
        o_f, o_b, bon_f, bon_b = _wkv(zs, zs_first if l > 0 else None, ups, vecs, batch, seq, d)
        if l == 0:
            zs_first = zs
        y_a = _rwkv_out(o_f, o_b, bon_f, bon_b, zs, _pad_rows(gate_up[l], GROUP_LANES).astype(BF16),
                        ln_x_w[l].reshape(1, d), ln_x_b[l].reshape(1, d), d)

        lam_init = 0.8 - 0.6 * math.exp(-0.3 * l)
        lam_vecs = jnp.stack([lambda_q1[l], lambda_k1[l], lambda_q2[l], lambda_k2[l]])
        y_b = _diff_attention(z_att, cos, sin_signed, lam_vecs, subln_w[l].reshape(1, -1),
                              batch, seq, d, lam_init)

        merged = _merge(y_a, y_b, proj_a[l].astype(BF16), proj_b[l].astype(BF16), z_att, d)
        x2 = _mm_residual_ln(merged, w_out[l].astype(BF16), x2, mod_l, 2,
                             ln1_g[l].reshape(1, d), ln1_b[l].reshape(1, d), seq, alpha)

        h = _ffn_up(x2, mod_l, ffn_w_gate[l].astype(BF16), ffn_w_up[l].astype(BF16), seq)
        x2 = _mm_residual_ln(h, ffn_w_down[l].astype(BF16), x2, mod_l, 5,
                             ln2_g[l].reshape(1, d), ln2_b[l].reshape(1, d), seq, alpha)

    return x2.reshape(batch, seq, d)
```

```python
import functools
import math

import numpy as np
import jax
import jax.numpy as jnp
from jax import lax
from jax.experimental import pallas as pl
from jax.experimental.pallas import tpu as pltpu

F32 = jnp.float32
BF16 = jnp.bfloat16
HIGHEST = lax.Precision.HIGHEST

RWKV_HEAD = 64
DIFF_HEAD = 128
ROPE_THETA = 10000.0
SUBLN_EPS = 1e-5
LN_EPS = 1e-5
GN_EPS = 1e-5 * RWKV_HEAD
EXP_NEG_HALF = math.exp(-0.5)
LOG2E = math.log2(math.e)

LANES = 128
MXU_DIM = 256
VMEM_LIMIT = 56 * 1024 * 1024

CHUNK = 64
HEADS_PER_GROUP = MXU_DIM // RWKV_HEAD
GROUP_LANES = HEADS_PER_GROUP * RWKV_HEAD
STACK_ROWS = HEADS_PER_GROUP * CHUNK
MAX_GROUPS_PER_STEP = 4
CHUNKS_PER_STEP = 2
STAGE_LAG = 4
STATE_BARRIER = "state-barrier"
CHAIN_DONE = "chain-done"
HALO = 16


def _sigmoid(x):
    return 1.0 / (1.0 + jnp.exp(-x))


def _dot(a, b, precision=None):
    return jnp.dot(a, b, preferred_element_type=F32, precision=precision)


def _dot_nt(a, b):
    return lax.dot_general(a, b, (((1,), (1,)), ((), ())), preferred_element_type=F32)


def _dot_tn(a, b):
    return lax.dot_general(a, b, (((0,), (0,)), ((), ())), preferred_element_type=F32)


def _split2(x):
    hi = x.astype(BF16)
    return hi, (x - hi.astype(F32)).astype(BF16)


def _params(*semantics):
    return pltpu.CompilerParams(dimension_semantics=semantics, vmem_limit_bytes=VMEM_LIMIT)


def _pick_tile(n, target, quantum):
    best = None
    t = quantum
    while t <= min(n, target):
        if n % t == 0:
            best = t
        t += quantum
    assert best is not None, (n, target, quantum)
    return best


def _mod_kernel(c_ref, w_ref, b_ref, o_ref):
    c = c_ref[...]
    c_act = c * _sigmoid(c)
    o_ref[...] = _dot(c_act, w_ref[...], HIGHEST) + b_ref[...]


def _modulation(c, ada_w, ada_b):
    nl, d, n6 = ada_w.shape
    b = c.shape[0]
    rows = -(-b // 8) * 8
    c_pad = jnp.pad(c, ((0, rows - b), (0, 0)))
    tn = _pick_tile(n6, 2048, LANES)
    out = pl.pallas_call(
        _mod_kernel,
        out_shape=jax.ShapeDtypeStruct((nl, rows, n6), F32),
        grid=(nl, n6 // tn),
        in_specs=[
            pl.BlockSpec((rows, d), lambda l, j: (0, 0)),
            pl.BlockSpec((None, d, tn), lambda l, j: (l, 0, j)),
            pl.BlockSpec((None, 1, tn), lambda l, j: (l, 0, j)),
        ],
        out_specs=pl.BlockSpec((None, rows, tn), lambda l, j: (l, 0, j)),
        compiler_params=_params("parallel", "parallel"),
        name="adaln_mod",
    )(c_pad, ada_w, ada_b.reshape(nl, 1, n6))
    return out[:, :b].reshape(nl, b, 6, d)


def _inproj_kernel(x_ref, xp_ref, xn_ref, mod_ref, w_ref, mu_ref, o_ref, u_ref, *acc_refs,
                   tm, tiles_per_seq, shift_row, scale_row):
    i = pl.program_id(0)
    j = pl.program_id(1)

    @pl.when(j == 0)
    def _():
        sc = 1.0 + mod_ref[scale_row:scale_row + 1, :]
        sh = mod_ref[shift_row:shift_row + 1, :]
        u_ref[0:HALO, :] = (xp_ref[...] * sc + sh).astype(BF16)
        u_ref[HALO:HALO + tm, :] = (x_ref[...] * sc + sh).astype(BF16)
        u_ref[HALO + tm:, :] = (xn_ref[...] * sc + sh).astype(BF16)

    row = lax.broadcasted_iota(jnp.int32, (tm, 1), 0)
    pos = i % tiles_per_seq
    first = jnp.logical_and(row == 0, pos == 0)
    last = jnp.logical_and(row == tm - 1, pos == tiles_per_seq - 1)

    def shift_out(acc_ref, cols):
        zc = acc_ref[HALO:HALO + tm, :]
        zp = jnp.where(first, 0.0, acc_ref[HALO - 1:HALO - 1 + tm, :])
        zn = jnp.where(last, 0.0, acc_ref[HALO + 1:HALO + 1 + tm, :])
        o_ref[:, cols] = (zc + mu_ref[0:1, cols] * (zp - zc) + mu_ref[1:2, cols] * (zn - zc)).astype(o_ref.dtype)

    sub = acc_refs[0].shape[1]
    parts = [slice(p * sub, (p + 1) * sub) for p in range(len(acc_refs))]
    for p, cols in enumerate(parts):
        acc_refs[p][...] = _dot(u_ref[...], w_ref[:, cols])
        if p > 0:
            shift_out(acc_refs[p - 1], parts[p - 1])
    shift_out(acc_refs[-1], parts[-1])


def _plain_inproj_kernel(x_ref, mod_ref, w_ref, o_ref, u_ref, *, shift_row, scale_row):
    @pl.when(pl.program_id(1) == 0)
    def _():
        sc = 1.0 + mod_ref[scale_row:scale_row + 1, :]
        sh = mod_ref[shift_row:shift_row + 1, :]
        u_ref[...] = (x_ref[...] * sc + sh).astype(BF16)

    o_ref[...] = _dot(u_ref[...], w_ref[...]).astype(o_ref.dtype)


def _inproj_shift(x2, mod_l, w, mu, seq, out_dtype):
    t, d = x2.shape
    n = w.shape[1]
    tm = _pick_tile(seq, 1024, HALO)
    tn = _pick_tile(n, 1024, MXU_DIM)
    sub = MXU_DIM
    tps = seq // tm
    hb = tm // HALO
    last_halo = t // HALO - 1
    kern = functools.partial(_inproj_kernel, tm=tm, tiles_per_seq=tps, shift_row=0, scale_row=1)
    return pl.pallas_call(
        kern,
        out_shape=jax.ShapeDtypeStruct((t, n), out_dtype),
        grid=(t // tm, n // tn),
        in_specs=[
            pl.BlockSpec((tm, d), lambda i, j: (i, 0)),
            pl.BlockSpec((HALO, d), lambda i, j: (jnp.maximum(i * hb - 1, 0), 0)),
            pl.BlockSpec((HALO, d), lambda i, j: (jnp.minimum((i + 1) * hb, last_halo), 0)),
            pl.BlockSpec((None, 6, d), lambda i, j: (i // tps, 0, 0)),
            pl.BlockSpec((d, tn), lambda i, j: (0, j)),
            pl.BlockSpec((2, tn), lambda i, j: (0, j)),
        ],
        out_specs=pl.BlockSpec((tm, tn), lambda i, j: (i, j)),
        scratch_shapes=[pltpu.VMEM((tm + 2 * HALO, d), BF16)]
        + [pltpu.VMEM((tm + 2 * HALO, sub), F32) for _ in range(tn // sub)],
        compiler_params=_params("parallel", "arbitrary"),
        name="inproj_shift",
    )(x2, x2, x2, mod_l, w, mu)


def _inproj_plain(x2, mod_l, w, seq, out_dtype):
    t, d = x2.shape
    n = w.shape[1]
    tm = _pick_tile(seq, 1024, HALO)
    tn = _pick_tile(n, 1024, LANES)
    tps = seq // tm
    kern = functools.partial(_plain_inproj_kernel, shift_row=0, scale_row=1)
    return pl.pallas_call(
        kern,
        out_shape=jax.ShapeDtypeStruct((t, n), out_dtype),
        grid=(t // tm, n // tn),
        in_specs=[
            pl.BlockSpec((tm, d), lambda i, j: (i, 0)),
            pl.BlockSpec((None, 6, d), lambda i, j: (i // tps, 0, 0)),
            pl.BlockSpec((d, tn), lambda i, j: (0, j)),
        ],
        out_specs=pl.BlockSpec((tm, tn), lambda i, j: (i, j)),
        scratch_shapes=[pltpu.VMEM((tm, d), BF16)],
        compiler_params=_params("parallel", "arbitrary"),
        name="inproj_plain",
    )(x2, mod_l, w)


def _wkv_constants():
    c, gc, gl = CHUNK, STACK_ROWS, GROUP_LANES
    assert gc == gl
    t = np.arange(c)
    lower = t[None, :] <= t[:, None]
    col_t = np.arange(gc) % c
    row_h = np.arange(gc) // c
    lane_h = np.arange(gl) // RWKV_HEAD
    tri = np.zeros((gc, gc), bool)
    tri[0:c, 0:c] = lower
    tri[c:2 * c, 0:c] = lower.T
    bd_lanes = lane_h[:, None] == lane_h[None, :]
    slab_h = np.stack([row_h[:, None] == lane_h[None, :], row_h[:, None] == row_h[None, :], bd_lanes, tri])
    slab_f = np.concatenate([
        col_t[None, :] < t[:, None], col_t[None, :] > t[:, None],
        col_t[None, :] <= t[:, None], col_t[None, :] >= t[:, None],
        col_t[None, :] == t[:, None], bd_lanes])
    return jnp.asarray(slab_h, BF16), jnp.asarray(slab_f, F32)


def _groups_per_step(d):
    groups = d // GROUP_LANES
    return max(g for g in range(1, MAX_GROUPS_PER_STEP + 1) if groups % g == 0)


def _wkv_chain(d, gi, gps, rows, zrkv, th, xab, vrb, v0, ups, vecs, cbf, cf, q_ref, o_ref, bon_ref,
               has_vres, state_ready):
    c, g, gc, gl = CHUNK, HEADS_PER_GROUP, STACK_ROWS, GROUP_LANES
    w = gps * gl
    sl = slice(gi * gl, (gi + 1) * gl)
    r = zrkv[rows, gi * gl:(gi + 1) * gl]
    k = zrkv[rows, w + gi * gl:w + (gi + 1) * gl]
    v = zrkv[rows, 2 * w + gi * gl:2 * w + (gi + 1) * gl]
    w0, a0 = vecs[d:d + 1, sl], vecs[2 + d:3 + d, sl]
    k_k, k_a, r_k = vecs[4:5, sl], vecs[5:6, sl], vecs[6:7, sl]
    sm, bdm, bdl = cbf[0], cbf[1], cbf[2]
    tri = cbf[3, d * c:(d + 1) * c, 0:c]
    ms = cf[d * c:(d + 1) * c, :] > 0.5
    mi = cf[(2 + d) * c:(3 + d) * c, :] > 0.5
    eye_w = cf[4 * c:5 * c, :]

    wl = _dot(th, ups[d, :, sl])
    al = _dot(xab, ups[2 + d, :, sl])
    if has_vres:
        ml = _dot(vrb, ups[4, :, sl])
    yield
    if has_vres:
        v = v + (v0[rows, sl] - v) * _sigmoid(vecs[7:8, sl] + ml)
    lw = -EXP_NEG_HALF * _sigmoid(w0 + wl)
    a = _sigmoid(a0 + al)
    kd = k * (1.0 + (a - 1.0) * k_a)
    kk0 = k * k_k
    sums = _dot(jnp.concatenate([*_split2(kk0 * kk0), *_split2(r * kd * r_k)], axis=0), bdl)
    cum2 = _dot(tri, jnp.concatenate(_split2(lw), axis=1))
    yield
    kk = kk0 * lax.rsqrt(jnp.maximum(sums[0:c] + sums[c:2 * c], 1e-24))
    bon_ref[rows, sl] = ((sums[2 * c:3 * c] + sums[3 * c:4 * c]) * v).astype(bon_ref.dtype)
    cum = cum2[:, 0:gl] + cum2[:, gl:2 * gl]
    tot = jnp.sum(lw, axis=0, keepdims=True)
    g_inv = jnp.exp(-cum)
    b = kk * a
    a_t = -(kk * jnp.exp(cum - lw))
    b_t = b * g_inv
    k_t = kd * g_inv
    r_t = r * jnp.exp(cum)
    g_rem = jnp.exp(tot - cum)
    bk_g = jnp.concatenate([b * g_rem, kd * g_rem], axis=0).astype(BF16)

    def stack(x):
        return jnp.concatenate([x.astype(BF16)] * g, axis=0) * sm

    def to_bd(xw):
        return jnp.concatenate([xw.astype(BF16)] * g, axis=0) * bdm

    ar = jnp.concatenate([a_t, r_t], axis=0).astype(BF16)
    nn = _dot_nt(ar, jnp.concatenate([stack(b_t), stack(k_t)], axis=0))
    if state_ready:
        q = q_ref[...]
        arq = _dot_nt(ar, q.astype(BF16))
    yield
    n_ab = jnp.where(ms, nn[0:c, 0:gc], 0.0)
    n_ak = jnp.where(ms, nn[0:c, gc:2 * gc], 0.0)
    n_rb = jnp.where(mi, nn[c:2 * c, 0:gc], 0.0).astype(BF16)
    n_rk = jnp.where(mi, nn[c:2 * c, gc:2 * gc], 0.0)
    pw = _dot(n_ab.astype(BF16), to_bd(n_ab))
    nv = _dot(jnp.concatenate([n_ak.astype(BF16), n_rk.astype(BF16)], axis=0), stack(v))
    yield
    inv = eye_w + n_ab
    steps = int(math.log2(c)) - 1
    for s in range(steps):
        rhs = to_bd(pw)
        if s < steps - 1:
            both = _dot(jnp.concatenate([pw.astype(BF16), inv.astype(BF16)], axis=0), rhs)
            yield
            pw = both[0:c]
            inv = inv + both[c:2 * c]
        else:
            last = _dot(inv.astype(BF16), rhs)
            yield
            inv = inv + last
    if not state_ready:
        yield STATE_BARRIER
        q = q_ref[...]
        arq = _dot_nt(ar, q.astype(BF16))
        yield
    y = arq[0:c] + nv[0:c]
    u = _dot(inv.astype(BF16), stack(y))
    yield
    o_u = _dot(n_rb, stack(u))
    upd = _dot_tn(jnp.concatenate([u, v], axis=0).astype(BF16), bk_g)
    yield
    o_ref[rows, sl] = (arq[c:2 * c] + nv[c:2 * c] + o_u).astype(o_ref.dtype)
    q_ref[...] = (q * jnp.exp(tot) + upd) * cf[5 * c:5 * c + gl, :]


def _wkv_kernel(*refs, has_vres, gps):
    n_act = 4 if has_vres else 2
    acts = [list(refs[0:n_act]), list(refs[n_act:2 * n_act])]
    ups, vecs, cbf, cf = refs[2 * n_act:2 * n_act + 4]
    outs = refs[2 * n_act + 4:2 * n_act + 8]
    q_refs = refs[2 * n_act + 8]

    @pl.when(pl.program_id(2) == 0)
    def _():
        q_refs[...] = jnp.zeros_like(q_refs)

    waves = []
    for sub in range(CHUNKS_PER_STEP):
        wave = []
        for d in range(2):
            pos = CHUNKS_PER_STEP - 1 - sub if d == 1 else sub
            rows = slice(pos * CHUNK, (pos + 1) * CHUNK)
            zrkv, xwa = acts[d][0:2]
            v0, vr = (acts[d][2], acts[d][3]) if has_vres else (None, None)
            th = jnp.tanh(xwa[rows, 0:LANES]).astype(BF16)
            xab = xwa[rows, LANES:2 * LANES].astype(BF16)
            vrb = vr[rows, :].astype(BF16) if has_vres else None
            for gi in range(gps):
                wave.append(_wkv_chain(d, gi, gps, rows, zrkv, th, xab, vrb, v0, ups, vecs, cbf, cf,
                                       q_refs.at[d * gps + gi], outs[d], outs[2 + d], has_vres, sub == 0))
        waves.append(wave)
    done = [[False] * len(wave) for wave in waves]
    held = [[False] * len(wave) for wave in waves]
    rounds = 0
    while not all(all(flags) for flags in done):
        for wi, wave in enumerate(waves):
            if rounds < wi * STAGE_LAG:
                continue
            for ci, chain in enumerate(wave):
                if done[wi][ci] or (held[wi][ci] and not done[wi - 1][ci]):
                    continue
                held[wi][ci] = False
                signal = next(chain, CHAIN_DONE)
                if signal is CHAIN_DONE:
                    done[wi][ci] = True
                elif signal is STATE_BARRIER:
                    held[wi][ci] = True
        rounds += 1


def _wkv(zs, zs_first, ups, vecs, batch, seq, d):
    t = zs.shape[0]
    has_vres = zs_first is not None
    c, gl = CHUNKS_PER_STEP * CHUNK, GROUP_LANES
    gps = _groups_per_step(d)
    w = gps * gl
    assert seq % c == 0 and d % w == 0
    nc = seq // c
    ng = d // w
    lora_pair0 = 3 * d // (2 * LANES)
    vres_block = (3 * d + 4 * LANES + gl) // LANES

    def dir_specs(rev):
        rw = (lambda b, i: b * nc + (nc - 1 - i)) if rev else (lambda b, i: b * nc + i)
        specs = [
            pl.BlockSpec((c, 3 * w), lambda b, g, i: (rw(b, i), g)),
            pl.BlockSpec((c, 2 * LANES), lambda b, g, i: (rw(b, i), lora_pair0 + rev)),
        ]
        args = [zs, zs]
        if has_vres:
            specs += [
                pl.BlockSpec((c, w), lambda b, g, i: (rw(b, i), 3 * g + 2)),
                pl.BlockSpec((c, LANES), lambda b, g, i: (rw(b, i), vres_block)),
            ]
            args += [zs_first, zs]
        return specs, args

    fs, fa = dir_specs(0)
    bs, ba = dir_specs(1)
    cbf, cf = _wkv_constants()
    p_specs = [
        pl.BlockSpec((5, LANES, w), lambda b, g, i: (0, 0, g)),
        pl.BlockSpec((8, w), lambda b, g, i: (0, g)),
        pl.BlockSpec(cbf.shape, lambda b, g, i: (0, 0, 0)),
        pl.BlockSpec(cf.shape, lambda b, g, i: (0, 0)),
    ]
    out_f = pl.BlockSpec((c, w), lambda b, g, i: (b * nc + i, g))
    out_b = pl.BlockSpec((c, w), lambda b, g, i: (b * nc + (nc - 1 - i), g))
    shp = jax.ShapeDtypeStruct((t, d), BF16)
    return pl.pallas_call(
        functools.partial(_wkv_kernel, has_vres=has_vres, gps=gps),
        out_shape=[shp, shp, shp, shp],
        grid=(batch, ng, nc),
        in_specs=fs + bs + p_specs,
        out_specs=[out_f, out_b, out_f, out_b],
        scratch_shapes=[pltpu.VMEM((2 * gps, gl, gl), F32)],
        compiler_params=_params("parallel", "parallel", "arbitrary"),
        name="wkv7_chunked",
    )(*fa, *ba, ups, vecs, cbf, cf)


def _rwkv_out_kernel(of_ref, ob_ref, bf_ref, bb_ref, xg_ref, gup_ref, lnw_ref, lnb_ref, bd_ref, o_ref):
    o = of_ref[...].astype(F32) + ob_ref[...].astype(F32)
    bd = bd_ref[...]
    inv_n = 1.0 / RWKV_HEAD
    tm = o.shape[0]

    def head_sum(x):
        s2 = _dot(jnp.concatenate(_split2(x), axis=0), bd)
        return s2[0:tm] + s2[tm:2 * tm]

    mu = head_sum(o) * inv_n
    oc = o - mu
    var = head_sum(oc * oc) * inv_n
    o_n = oc * lax.rsqrt(var + GN_EPS) * lnw_ref[...] + lnb_ref[...]
    gate = _dot(_sigmoid(xg_ref[...]).astype(BF16), gup_ref[...])
    bonus = bf_ref[...].astype(F32) + bb_ref[...].astype(F32)
    o_ref[...] = ((o_n + bonus) * gate).astype(o_ref.dtype)


def _rwkv_out(o_f, o_b, bon_f, bon_b, zs, gate_up, ln_w, ln_b, d):
    t = o_f.shape[0]
    gl = GROUP_LANES
    tm = _pick_tile(t, 2048, 8)
    xg_block = (3 * d + 4 * LANES) // gl
    bd = _wkv_constants()[0][2]
    big = pl.BlockSpec((tm, gl), lambda i, j: (i, j))
    return pl.pallas_call(
        _rwkv_out_kernel,
        out_shape=jax.ShapeDtypeStruct((t, d), BF16),
        grid=(t // tm, d // gl),
        in_specs=[
            big, big, big, big,
            pl.BlockSpec((tm, gl), lambda i, j: (i, xg_block)),
            pl.BlockSpec((gl, gl), lambda i, j: (0, j)),
            pl.BlockSpec((1, gl), lambda i, j: (0, j)),
            pl.BlockSpec((1, gl), lambda i, j: (0, j)),
            pl.BlockSpec((gl, gl), lambda i, j: (0, 0)),
        ],
        out_specs=big,
        compiler_params=_params("parallel", "parallel"),
        name="rwkv_out",
    )(o_f, o_b, bon_f, bon_b, zs, gate_up, ln_w, ln_b, bd)


def _attn_kernel(q_ref, k_ref, v_ref, cos_ref, sin_ref, lam_ref, sw_ref, o_ref, kr_ref, vt_ref, *,
                 tq, tqs, tk, lam_init):
    qi = pl.program_id(2)
    dh = DIFF_HEAD

    @pl.when(qi == 0)
    def _():
        blk = 512 if k_ref.shape[0] % 512 == 0 else k_ref.shape[0]
        for c0 in range(0, k_ref.shape[0], blk):
            cos = cos_ref[c0:c0 + blk, :]
            sin = sin_ref[c0:c0 + blk, :]
            for m in range(2):
                km = k_ref[c0:c0 + blk, m * dh:(m + 1) * dh].astype(F32)
                kr_ref[c0:c0 + blk, m * dh:(m + 1) * dh] = (
                    km * cos + pltpu.roll(km, dh // 2, 1) * sin).astype(BF16)
            vt_ref[:, c0:c0 + blk] = v_ref[c0:c0 + blk, :].astype(F32).T.astype(BF16)

    lam = (jnp.exp(jnp.sum(lam_ref[0:1, :] * lam_ref[1:2, :], axis=-1, keepdims=True))
           - jnp.exp(jnp.sum(lam_ref[2:3, :] * lam_ref[3:4, :], axis=-1, keepdims=True)) + lam_init)
    nkv = k_ref.shape[0] // tk
    for q0 in range(0, tq, tqs):
        row0 = pl.multiple_of(qi * tq + q0, tqs)
        cq = cos_ref[pl.ds(row0, tqs), :]
        sq = sin_ref[pl.ds(row0, tqs), :]
        qs = []
        for m in range(2):
            qm = q_ref[q0:q0 + tqs, m * dh:(m + 1) * dh].astype(F32)
            qm = (qm * cq + pltpu.roll(qm, dh // 2, 1) * sq) * (dh ** -0.5 * LOG2E)
            qs.append(qm.T.astype(BF16))
        mx, ls, acc = [None, None], [None, None], [None, None]
        for j in range(nkv):
            vt = vt_ref[:, j * tk:(j + 1) * tk]
            for m in range(2):
                st = _dot(kr_ref[j * tk:(j + 1) * tk, m * dh:(m + 1) * dh], qs[m])
                tile_max = jnp.max(st, axis=0, keepdims=True)
                m_new = tile_max if j == 0 else jnp.maximum(mx[m], tile_max)
                p = jnp.exp2(st - m_new)
                psum = jnp.sum(p, axis=0, keepdims=True)
                pv = _dot(vt, p.astype(BF16))
                if j == 0:
                    ls[m], acc[m] = psum, pv
                else:
                    alpha = jnp.exp2(mx[m] - m_new)
                    ls[m] = alpha * ls[m] + psum
                    acc[m] = alpha * acc[m] + pv
                mx[m] = m_new
        o = (acc[0] * (1.0 / ls[0]) - acc[1] * (lam / ls[1])).T
        o = o * lax.rsqrt(jnp.mean(o * o, axis=-1, keepdims=True) + SUBLN_EPS) * sw_ref[...]
        o_ref[q0:q0 + tqs, :] = (o * (1.0 - lam_init)).astype(o_ref.dtype)


def _diff_attention(z_att, cos, sin_signed, lam_vecs, subln_w, batch, seq, d, lam_init):
    t = z_att.shape[0]
    hw = 2 * DIFF_HEAD
    nh = d // hw
    tqs = _pick_tile(seq, MXU_DIM, 8)
    tq = 2 * tqs if seq % (2 * tqs) == 0 else tqs
    nq = seq // tq
    tk = _pick_tile(seq, MXU_DIM, LANES)
    kern = functools.partial(_attn_kernel, tq=tq, tqs=tqs, tk=tk, lam_init=lam_init)
    return pl.pallas_call(
        kern,
        out_shape=jax.ShapeDtypeStruct((t, d), BF16),
        grid=(batch, nh, nq),
        in_specs=[
            pl.BlockSpec((tq, hw), lambda b, h, i: (b * nq + i, h)),
            pl.BlockSpec((seq, hw), lambda b, h, i: (b, nh + h)),
            pl.BlockSpec((seq, hw), lambda b, h, i: (b, 2 * nh + h)),
            pl.BlockSpec((seq, DIFF_HEAD), lambda b, h, i: (0, 0)),
            pl.BlockSpec((seq, DIFF_HEAD), lambda b, h, i: (0, 0)),
            pl.BlockSpec((4, DIFF_HEAD), lambda b, h, i: (0, 0)),
            pl.BlockSpec((1, hw), lambda b, h, i: (0, 0)),
        ],
        out_specs=pl.BlockSpec((tq, hw), lambda b, h, i: (b * nq + i, h)),
        scratch_shapes=[pltpu.VMEM((seq, hw), BF16), pltpu.VMEM((hw, seq), BF16)],
        compiler_params=_params("parallel", "parallel", "arbitrary"),
        name="diff_attention",
    )(z_att, z_att, z_att, cos, sin_signed, lam_vecs, subln_w)


def _merge_kernel(ya_ref, yb_ref, pa_ref, pb_ref, ga_ref, gb_ref, o_ref):
    ma = _dot(ya_ref[...], pa_ref[...])
    mb = _dot(yb_ref[...], pb_ref[...])
    ga = _sigmoid(ga_ref[...].astype(F32))
    gb = _sigmoid(gb_ref[...].astype(F32))
    o_ref[...] = (ga * ma + gb * mb).astype(o_ref.dtype)


def _merge(y_a, y_b, proj_a, proj_b, z_att, d):
    t = y_a.shape[0]
    tm = _pick_tile(t, 1024, 8)
    tn = _pick_tile(d, 512, LANES)
    nb = d // tn
    return pl.pallas_call(
        _merge_kernel,
        out_shape=jax.ShapeDtypeStruct((t, d), BF16),
        grid=(t // tm, nb),
        in_specs=[
            pl.BlockSpec((tm, d), lambda i, j: (i, 0)),
            pl.BlockSpec((tm, d), lambda i, j: (i, 0)),
            pl.BlockSpec((d, tn), lambda i, j: (0, j)),
            pl.BlockSpec((d, tn), lambda i, j: (0, j)),
            pl.BlockSpec((tm, tn), lambda i, j: (i, 3 * nb + j)),
            pl.BlockSpec((tm, tn), lambda i, j: (i, 4 * nb + j)),
        ],
        out_specs=pl.BlockSpec((tm, tn), lambda i, j: (i, j)),
        compiler_params=_params("parallel", "arbitrary"),
        name="gated_merge",
    )(y_a, y_b, proj_a, proj_b, z_att, z_att)


def _mm_ln_kernel(a_ref, w_ref, x_ref, mod_ref, g_ref, b_ref, o_ref, *, nk, gate_row, alpha):
    kk = pl.program_id(1)
    tm = o_ref.shape[0]
    halves = [slice(0, tm // 2), slice(tm // 2, tm)]

    def layer_norm_rows(rows, acc):
        y = alpha * x_ref[rows, :] + mod_ref[gate_row:gate_row + 1, :] * acc
        mu = jnp.mean(y, axis=-1, keepdims=True)
        yc = y - mu
        var = jnp.mean(yc * yc, axis=-1, keepdims=True)
        o_ref[rows, :] = yc * lax.rsqrt(var + LN_EPS) * g_ref[...] + b_ref[...]

    def finish(accumulated):
        w = w_ref[...]
        acc0 = _dot(a_ref[halves[0], :], w)
        acc1 = _dot(a_ref[halves[1], :], w)
        if accumulated:
            acc0 = acc0 + o_ref[halves[0], :]
        layer_norm_rows(halves[0], acc0)
        if accumulated:
            acc1 = acc1 + o_ref[halves[1], :]
        layer_norm_rows(halves[1], acc1)

    if nk == 1:
        finish(False)
        return

    @pl.when(kk == 0)
    def _():
        o_ref[...] = _dot(a_ref[...], w_ref[...])

    @pl.when(jnp.logical_and(kk > 0, kk < nk - 1))
    def _():
        o_ref[...] += _dot(a_ref[...], w_ref[...])

    @pl.when(kk == nk - 1)
    def _():
        finish(True)


def _mm_residual_ln(a, w, x2, mod_l, gate_row, ln_g, ln_b, seq, alpha):
    t, kdim = a.shape
    d = w.shape[1]
    if kdim <= 2048:
        tm, tk = _pick_tile(seq, 512, 8), kdim
    else:
        tm, tk = _pick_tile(seq, 1024, 8), _pick_tile(kdim, 768, LANES)
    nk = kdim // tk
    tps = seq // tm
    kern = functools.partial(_mm_ln_kernel, nk=nk, gate_row=gate_row, alpha=alpha)
    return pl.pallas_call(
        kern,
        out_shape=jax.ShapeDtypeStruct((t, d), F32),
        grid=(t // tm, nk),
        in_specs=[
            pl.BlockSpec((tm, tk), lambda i, k: (i, k)),
            pl.BlockSpec((tk, d), lambda i, k: (k, 0)),
            pl.BlockSpec((tm, d), lambda i, k: (i, 0)),
            pl.BlockSpec((None, 6, d), lambda i, k: (i // tps, 0, 0)),
            pl.BlockSpec((1, d), lambda i, k: (0, 0)),
            pl.BlockSpec((1, d), lambda i, k: (0, 0)),
        ],
        out_specs=pl.BlockSpec((tm, d), lambda i, k: (i, 0)),
        compiler_params=_params("parallel", "arbitrary"),
        name="matmul_residual_ln",
    )(a, w, x2, mod_l, ln_g, ln_b)


def _ffn_up_kernel(x_ref, mod_ref, wg_ref, wu_ref, o_ref, u_ref, *, shift_row, scale_row):
    @pl.when(pl.program_id(1) == 0)
    def _():
        sc = 1.0 + mod_ref[scale_row:scale_row + 1, :]
        sh = mod_ref[shift_row:shift_row + 1, :]
        u_ref[...] = (x_ref[...] * sc + sh).astype(BF16)

    u = u_ref[...]
    half = o_ref.shape[1] // 2
    halves = [slice(0, half), slice(half, 2 * half)]
    prods = [(_dot(u, wg_ref[:, cols]), _dot(u, wu_ref[:, cols])) for cols in halves]
    for cols, (hg, hu) in zip(halves, prods):
        o_ref[:, cols] = (hg * _sigmoid(hg) * hu).astype(o_ref.dtype)


def _ffn_up(x2, mod_l, w_gate, w_up, seq):
    t, d = x2.shape
    f = w_gate.shape[1]
    tm = _pick_tile(seq, 1024, 8)
    tn = _pick_tile(f, 512, LANES)
    tps = seq // tm
    kern = functools.partial(_ffn_up_kernel, shift_row=3, scale_row=4)
    return pl.pallas_call(
        kern,
        out_shape=jax.ShapeDtypeStruct((t, f), BF16),
        grid=(t // tm, f // tn),
        in_specs=[
            pl.BlockSpec((tm, d), lambda i, j: (i, 0)),
            pl.BlockSpec((None, 6, d), lambda i, j: (i // tps, 0, 0)),
            pl.BlockSpec((d, tn), lambda i, j: (0, j)),
            pl.BlockSpec((d, tn), lambda i, j: (0, j)),
        ],
        out_specs=pl.BlockSpec((tm, tn), lambda i, j: (i, j)),
        scratch_shapes=[pltpu.VMEM((tm, d), BF16)],
        compiler_params=_params("parallel", "arbitrary"),
        name="ffn_up",
    )(x2, mod_l, w_gate, w_up)


def _pad_cols(a, width):
    return jnp.pad(a, [(0, 0)] * (a.ndim - 1) + [(0, width - a.shape[-1])])


def _pad_rows(a, height):
    return jnp.pad(a, [(0, 0)] * (a.ndim - 2) + [(0, height - a.shape[-2]), (0, 0)])


def _rope_tables(seq):
    pos = jnp.arange(seq, dtype=F32)
    inv = ROPE_THETA ** (-jnp.arange(0, DIFF_HEAD, 2, dtype=F32) / DIFF_HEAD)
    ang = pos[:, None] * inv[None, :]
    emb = jnp.concatenate([ang, ang], axis=-1)
    sign = jnp.where(jnp.arange(DIFF_HEAD) < DIFF_HEAD // 2, -1.0, 1.0).astype(F32)
    return jnp.cos(emb), jnp.sin(emb) * sign[None, :]


def kernel(x, c, ada_w, ada_b, w_in, shift_mu_prev, shift_mu_next, decay_w0, decay_up, iclr_a0, iclr_up, gate_up, k_k, k_a, r_k, ln_x_w, ln_x_b, vres_down, vres_up, vres_v0, lambda_q1, lambda_k1, lambda_q2, lambda_k2, subln_w, proj_a, proj_b, w_out, ln1_g, ln1_b, ffn_w_gate, ffn_w_up, ffn_w_down, ln2_g, ln2_b):
    batch, seq, d = x.shape
    depth = ada_w.shape[0]
    t = batch * seq
    dl, il, gl_rank = decay_up.shape[2], iclr_up.shape[2], gate_up.shape[1]
    vl = vres_down.shape[2]
    assert max(dl, il, vl) <= LANES and gl_rank <= GROUP_LANES
    rwkv_cols = 3 * d + 2 * dl + 2 * il + gl_rank
    alpha = (2 * depth) ** 0.25

    mod = _modulation(c, ada_w, ada_b)
    cos, sin_signed = _rope_tables(seq)
    x2 = x.reshape(t, d)
    zs_first = None

    for l in range(depth):
        mod_l = mod[l]
        w_l = w_in[l]
        gw = _groups_per_step(d) * GROUP_LANES
        cuts = np.cumsum([3 * d, dl, dl, il, il, gl_rank])

        def regroup(a):
            rkv, xw_f, xw_b, xa_f, xa_b, xg = jnp.split(a[..., :rwkv_cols], cuts[:-1], axis=-1)
            blocks = [rkv[..., p * d + g * gw:p * d + (g + 1) * gw] for g in range(d // gw) for p in range(3)]
            return blocks + [_pad_cols(xw_f, LANES), _pad_cols(xa_f, LANES), _pad_cols(xw_b, LANES),
                             _pad_cols(xa_b, LANES), _pad_cols(xg, GROUP_LANES)]

        w_parts = [p.astype(BF16) for p in regroup(w_l)]
        if l > 0:
            w_parts.append(_pad_cols(vres_down[l - 1].astype(BF16), LANES))
        else:
            w_parts.append(jnp.zeros((d, LANES), BF16))
        n_rwkv = sum(p.shape[1] for p in w_parts)
        n_pad = -(-n_rwkv // 1024) * 1024
        w_rwkv = _pad_cols(jnp.concatenate(w_parts, axis=1), n_pad)
        mu = jnp.stack([
            _pad_cols(jnp.concatenate(regroup(shift_mu_prev[l]), axis=0), n_pad),
            _pad_cols(jnp.concatenate(regroup(shift_mu_next[l]), axis=0), n_pad)])
        w_att = w_l[:, rwkv_cols:].astype(BF16)

        zs = _inproj_shift(x2, mod_l, w_rwkv, mu, seq, F32)
        z_att = _inproj_plain(x2, mod_l, w_att, seq, BF16)

        vup = _pad_rows(vres_up[l - 1], LANES) if l > 0 else jnp.zeros((LANES, d), F32)
        vv0 = vres_v0[l - 1] if l > 0 else jnp.zeros((d,), F32)
        ups = jnp.concatenate([_pad_rows(decay_up[l], LANES), _pad_rows(iclr_up[l], LANES), vup[None]],
                              axis=0).astype(BF16)
        vecs = jnp.stack([decay_w0[l, 0], decay_w0[l, 1], iclr_a0[l, 0], iclr_a0[l, 1],
                          k_k[l], k_a[l], r_k[l].reshape(d), vv0])
        o_f, o_b, bon_f, bon_b = _wkv(zs, zs_first if l > 0 else None, ups, vecs, batch, seq, d)
        if l == 0:
            zs_first = zs
        y_a = _rwkv_out(o_f, o_b, bon_f, bon_b, zs, _pad_rows(gate_up[l], GROUP_LANES).astype(BF16),
                        ln_x_w[l].reshape(1, d), ln_x_b[l].reshape(1, d), d)

        lam_init = 0.8 - 0.6 * math.exp(-0.3 * l)
        lam_vecs = jnp.stack([lambda_q1[l], lambda_k1[l], lambda_q2[l], lambda_k2[l]])
        y_b = _diff_attention(z_att, cos, sin_signed, lam_vecs, subln_w[l].reshape(1, -1),
                              batch, seq, d, lam_init)

        merged = _merge(y_a, y_b, proj_a[l].astype(BF16), proj_b[l].astype(BF16), z_att, d)
        x2 = _mm_residual_ln(merged, w_out[l].astype(BF16), x2, mod_l, 2,
                             ln1_g[l].reshape(1, d), ln1_b[l].reshape(1, d), seq, alpha)

        h = _ffn_up(x2, mod_l, ffn_w_gate[l].astype(BF16), ffn_w_up[l].astype(BF16), seq)
        x2 = _mm_residual_ln(h, ffn_w_down[l].astype(BF16), x2, mod_l, 5,
                             ln2_g[l].reshape(1, d), ln2_b[l].reshape(1, d), seq, alpha)

    return x2.reshape(batch, seq, d)
```

```python
import functools
import math

import numpy as np
import jax
import jax.numpy as jnp
from jax import lax
from jax.experimental import pallas as pl
from jax.experimental.pallas import tpu as pltpu

F32 = jnp.float32
BF16 = jnp.bfloat16
HIGHEST = lax.Precision.HIGHEST

RWKV_HEAD = 64
DIFF_HEAD = 128
ROPE_THETA = 10000.0
SUBLN_EPS = 1e-5
LN_EPS = 1e-5
GN_EPS = 1e-5 * RWKV_HEAD
EXP_NEG_HALF = math.exp(-0.5)
LOG2E = math.log2(math.e)

LANES = 128
MXU_DIM = 256
VMEM_LIMIT = 56 * 1024 * 1024

CHUNK = 64
HEADS_PER_GROUP = MXU_DIM // RWKV_HEAD
GROUP_LANES = HEADS_PER_GROUP * RWKV_HEAD
STACK_ROWS = HEADS_PER_GROUP * CHUNK
MAX_GROUPS_PER_STEP = 4
CHUNKS_PER_STEP = 2
STAGE_LAG = 4
STATE_BARRIER = "state-barrier"
CHAIN_DONE = "chain-done"
HALO = 16


def _sigmoid(x):
    return 1.0 / (1.0 + jnp.exp(-x))


def _dot(a, b, precision=None):
    return jnp.dot(a, b, preferred_element_type=F32, precision=precision)


def _dot_nt(a, b):
    return lax.dot_general(a, b, (((1,), (1,)), ((), ())), preferred_element_type=F32)


def _dot_tn(a, b):
    return lax.dot_general(a, b, (((0,), (0,)), ((), ())), preferred_element_type=F32)


def _split2(x):
    hi = x.astype(BF16)
    return hi, (x - hi.astype(F32)).astype(BF16)


def _params(*semantics):
    return pltpu.CompilerParams(dimension_semantics=semantics, vmem_limit_bytes=VMEM_LIMIT)


def _pick_tile(n, target, quantum):
    best = None
    t = quantum
    while t <= min(n, target):
        if n % t == 0:
            best = t
        t += quantum
    assert best is not None, (n, target, quantum)
    return best


def _mod_kernel(c_ref, w_ref, b_ref, o_ref):
    c = c_ref[...]
    c_act = c * _sigmoid(c)
    o_ref[...] = _dot(c_act, w_ref[...], HIGHEST) + b_ref[...]


def _modulation(c, ada_w, ada_b):
    nl, d, n6 = ada_w.shape
    b = c.shape[0]
    rows = -(-b // 8) * 8
    c_pad = jnp.pad(c, ((0, rows - b), (0, 0)))
    tn = _pick_tile(n6, 2048, LANES)
    out = pl.pallas_call(
        _mod_kernel,
        out_shape=jax.ShapeDtypeStruct((nl, rows, n6), F32),
        grid=(nl, n6 // tn),
        in_specs=[
            pl.BlockSpec((rows, d), lambda l, j: (0, 0)),
            pl.BlockSpec((None, d, tn), lambda l, j: (l, 0, j)),
            pl.BlockSpec((None, 1, tn), lambda l, j: (l, 0, j)),
        ],
        out_specs=pl.BlockSpec((None, rows, tn), lambda l, j: (l, 0, j)),
        compiler_params=_params("parallel", "parallel"),
        name="adaln_mod",
    )(c_pad, ada_w, ada_b.reshape(nl, 1, n6))
    return out[:, :b].reshape(nl, b, 6, d)


def _inproj_kernel(x_ref, xp_ref, xn_ref, mod_ref, w_ref, mu_ref, o_ref, u_ref, *acc_refs,
                   tm, tiles_per_seq, shift_row, scale_row):
    i = pl.program_id(0)
    j = pl.program_id(1)

    @pl.when(j == 0)
    def _():
        sc = 1.0 + mod_ref[scale_row:scale_row + 1, :]
        sh = mod_ref[shift_row:shift_row + 1, :]
        u_ref[0:HALO, :] = (xp_ref[...] * sc + sh).astype(BF16)
        u_ref[HALO:HALO + tm, :] = (x_ref[...] * sc + sh).astype(BF16)
        u_ref[HALO + tm:, :] = (xn_ref[...] * sc + sh).astype(BF16)

    row = lax.broadcasted_iota(jnp.int32, (tm, 1), 0)
    pos = i % tiles_per_seq
    first = jnp.logical_and(row == 0, pos == 0)
    last = jnp.logical_and(row == tm - 1, pos == tiles_per_seq - 1)

    def shift_out(acc_ref, cols):
        zc = acc_ref[HALO:HALO + tm, :]
        zp = jnp.where(first, 0.0, acc_ref[HALO - 1:HALO - 1 + tm, :])
        zn = jnp.where(last, 0.0, acc_ref[HALO + 1:HALO + 1 + tm, :])
        o_ref[:, cols] = (zc + mu_ref[0:1, cols] * (zp - zc) + mu_ref[1:2, cols] * (zn - zc)).astype(o_ref.dtype)

    sub = acc_refs[0].shape[1]
    parts = [slice(p * sub, (p + 1) * sub) for p in range(len(acc_refs))]
    for p, cols in enumerate(parts):
        acc_refs[p][...] = _dot(u_ref[...], w_ref[:, cols])
        if p > 0:
            shift_out(acc_refs[p - 1], parts[p - 1])
    shift_out(acc_refs[-1], parts[-1])


def _plain_inproj_kernel(x_ref, mod_ref, w_ref, o_ref, u_ref, *, shift_row, scale_row):
    @pl.when(pl.program_id(1) == 0)
    def _():
        sc = 1.0 + mod_ref[scale_row:scale_row + 1, :]
        sh = mod_ref[shift_row:shift_row + 1, :]
        u_ref[...] = (x_ref[...] * sc + sh).astype(BF16)

    o_ref[...] = _dot(u_ref[...], w_ref[...]).astype(o_ref.dtype)


def _inproj_shift(x2, mod_l, w, mu, seq, out_dtype):
    t, d = x2.shape
    n = w.shape[1]
    tm = _pick_tile(seq, 1024, HALO)
    tn = _pick_tile(n, 1024, MXU_DIM)
    sub = MXU_DIM
    tps = seq // tm
    hb = tm // HALO
    last_halo = t // HALO - 1
    kern = functools.partial(_inproj_kernel, tm=tm, tiles_per_seq=tps, shift_row=0, scale_row=1)
    return pl.pallas_call(
        kern,
        out_shape=jax.ShapeDtypeStruct((t, n), out_dtype),
        grid=(t // tm, n // tn),
        in_specs=[
            pl.BlockSpec((tm, d), lambda i, j: (i, 0)),
            pl.BlockSpec((HALO, d), lambda i, j: (jnp.maximum(i * hb - 1, 0), 0)),
            pl.BlockSpec((HALO, d), lambda i, j: (jnp.minimum((i + 1) * hb, last_halo), 0)),
            pl.BlockSpec((None, 6, d), lambda i, j: (i // tps, 0, 0)),
            pl.BlockSpec((d, tn), lambda i, j: (0, j)),
            pl.BlockSpec((2, tn), lambda i, j: (0, j)),
        ],
        out_specs=pl.BlockSpec((tm, tn), lambda i, j: (i, j)),
        scratch_shapes=[pltpu.VMEM((tm + 2 * HALO, d), BF16)]
        + [pltpu.VMEM((tm + 2 * HALO, sub), F32) for _ in range(tn // sub)],
        compiler_params=_params("parallel", "arbitrary"),
        name="inproj_shift",
    )(x2, x2, x2, mod_l, w, mu)


def _inproj_plain(x2, mod_l, w, seq, out_dtype):
    t, d = x2.shape
    n = w.shape[1]
    tm = _pick_tile(seq, 1024, HALO)
    tn = _pick_tile(n, 1024, LANES)
    tps = seq // tm
    kern = functools.partial(_plain_inproj_kernel, shift_row=0, scale_row=1)
    return pl.pallas_call(
        kern,
        out_shape=jax.ShapeDtypeStruct((t, n), out_dtype),
        grid=(t // tm, n // tn),
        in_specs=[
            pl.BlockSpec((tm, d), lambda i, j: (i, 0)),
            pl.BlockSpec((None, 6, d), lambda i, j: (i // tps, 0, 0)),
            pl.BlockSpec((d, tn), lambda i, j: (0, j)),
        ],
        out_specs=pl.BlockSpec((tm, tn), lambda i, j: (i, j)),
        scratch_shapes=[pltpu.VMEM((tm, d), BF16)],
        compiler_params=_params("parallel", "arbitrary"),
        name="inproj_plain",
    )(x2, mod_l, w)


def _wkv_constants():
    c, gc, gl = CHUNK, STACK_ROWS, GROUP_LANES
    assert gc == gl
    t = np.arange(c)
    lower = t[None, :] <= t[:, None]
    col_t = np.arange(gc) % c
    row_h = np.arange(gc) // c
    lane_h = np.arange(gl) // RWKV_HEAD
    tri = np.zeros((gc, gc), bool)
    tri[0:c, 0:c] = lower
    tri[c:2 * c, 0:c] = lower.T
    bd_lanes = lane_h[:, None] == lane_h[None, :]
    slab_h = np.stack([row_h[:, None] == lane_h[None, :], row_h[:, None] == row_h[None, :], bd_lanes, tri])
    slab_f = np.concatenate([
        col_t[None, :] < t[:, None], col_t[None, :] > t[:, None],
        col_t[None, :] <= t[:, None], col_t[None, :] >= t[:, None],
        col_t[None, :] == t[:, None], bd_lanes])
    return jnp.asarray(slab_h, BF16), jnp.asarray(slab_f, F32)


def _groups_per_step(d):
    groups = d // GROUP_LANES
    return max(g for g in range(1, MAX_GROUPS_PER_STEP + 1) if groups % g == 0)


def _wkv_chain(d, gi, gps, rows, zrkv, th, xab, vrb, v0, ups, vecs, cbf, cf, q_ref, o_ref, bon_ref,
               has_vres, state_ready):
    c, g, gc, gl = CHUNK, HEADS_PER_GROUP, STACK_ROWS, GROUP_LANES
    w = gps * gl
    sl = slice(gi * gl, (gi + 1) * gl)
    r = zrkv[rows, gi * gl:(gi + 1) * gl]
    k = zrkv[rows, w + gi * gl:w + (gi + 1) * gl]
    v = zrkv[rows, 2 * w + gi * gl:2 * w + (gi + 1) * gl]
    w0, a0 = vecs[d:d + 1, sl], vecs[2 + d:3 + d, sl]
    k_k, k_a, r_k = vecs[4:5, sl], vecs[5:6, sl], vecs[6:7, sl]
    sm, bdm, bdl = cbf[0], cbf[1], cbf[2]
    tri = cbf[3, d * c:(d + 1) * c, 0:c]
    ms = cf[d * c:(d + 1) * c, :] > 0.5
    mi = cf[(2 + d) * c:(3 + d) * c, :] > 0.5
    eye_w = cf[4 * c:5 * c, :]

    wl = _dot(th, ups[d, :, sl])
    al = _dot(xab, ups[2 + d, :, sl])
    if has_vres:
        ml = _dot(vrb, ups[4, :, sl])
    yield
    if has_vres:
        v = v + (v0[rows, sl] - v) * _sigmoid(vecs[7:8, sl] + ml)
    lw = -EXP_NEG_HALF * _sigmoid(w0 + wl)
    a = _sigmoid(a0 + al)
    kd = k * (1.0 + (a - 1.0) * k_a)
    kk0 = k * k_k
    sums = _dot(jnp.concatenate([*_split2(kk0 * kk0), *_split2(r * kd * r_k)], axis=0), bdl)
    cum2 = _dot(tri, jnp.concatenate(_split2(lw), axis=1))
    yield
    kk = kk0 * lax.rsqrt(jnp.maximum(sums[0:c] + sums[c:2 * c], 1e-24))
    bon_ref[rows, sl] = ((sums[2 * c:3 * c] + sums[3 * c:4 * c]) * v).astype(bon_ref.dtype)
    cum = cum2[:, 0:gl] + cum2[:, gl:2 * gl]
    tot = jnp.sum(lw, axis=0, keepdims=True)
    g_inv = jnp.exp(-cum)
    b = kk * a
    a_t = -(kk * jnp.exp(cum - lw))
    b_t = b * g_inv
    k_t = kd * g_inv
    r_t = r * jnp.exp(cum)
    g_rem = jnp.exp(tot - cum)
    bk_g = jnp.concatenate([b * g_rem, kd * g_rem], axis=0).astype(BF16)

    def stack(x):
        return jnp.concatenate([x.astype(BF16)] * g, axis=0) * sm

    def to_bd(xw):
        return jnp.concatenate([xw.astype(BF16)] * g, axis=0) * bdm

    ar = jnp.concatenate([a_t, r_t], axis=0).astype(BF16)
    nn = _dot_nt(ar, jnp.concatenate([stack(b_t), stack(k_t)], axis=0))
    if state_ready:
        q = q_ref[...]
        arq = _dot_nt(ar, q.astype(BF16))
    yield
    n_ab = jnp.where(ms, nn[0:c, 0:gc], 0.0)
    n_ak = jnp.where(ms, nn[0:c, gc:2 * gc], 0.0)
    n_rb = jnp.where(mi, nn[c:2 * c, 0:gc], 0.0).astype(BF16)
    n_rk = jnp.where(mi, nn[c:2 * c, gc:2 * gc], 0.0)
    pw = _dot(n_ab.astype(BF16), to_bd(n_ab))
    nv = _dot(jnp.concatenate([n_ak.astype(BF16), n_rk.astype(BF16)], axis=0), stack(v))
    yield
    inv = eye_w + n_ab
    steps = int(math.log2(c)) - 1
    for s in range(steps):
        rhs = to_bd(pw)
        if s < steps - 1:
            both = _dot(jnp.concatenate([pw.astype(BF16), inv.astype(BF16)], axis=0), rhs)
            yield
            pw = both[0:c]
            inv = inv + both[c:2 * c]
        else:
            last = _dot(inv.astype(BF16), rhs)
            yield
            inv = inv + last
    if not state_ready:
        yield STATE_BARRIER
        q = q_ref[...]
        arq = _dot_nt(ar, q.astype(BF16))
        yield
    y = arq[0:c] + nv[0:c]
    u = _dot(inv.astype(BF16), stack(y))
    yield
    o_u = _dot(n_rb, stack(u))
    upd = _dot_tn(jnp.concatenate([u, v], axis=0).astype(BF16), bk_g)
    yield
    o_ref[rows, sl] = (arq[c:2 * c] + nv[c:2 * c] + o_u).astype(o_ref.dtype)
    q_ref[...] = (q * jnp.exp(tot) + upd) * cf[5 * c:5 * c + gl, :]


def _wkv_kernel(*refs, has_vres, gps):
    n_act = 4 if has_vres else 2
    acts = [list(refs[0:n_act]), list(refs[n_act:2 * n_act])]
    ups, vecs, cbf, cf = refs[2 * n_act:2 * n_act + 4]
    outs = refs[2 * n_act + 4:2 * n_act + 8]
    q_refs = refs[2 * n_act + 8]

    @pl.when(pl.program_id(2) == 0)
    def _():
        q_refs[...] = jnp.zeros_like(q_refs)

    waves = []
    for sub in range(CHUNKS_PER_STEP):
        wave = []
        for d in range(2):
            pos = CHUNKS_PER_STEP - 1 - sub if d == 1 else sub
            rows = slice(pos * CHUNK, (pos + 1) * CHUNK)
            zrkv, xwa = acts[d][0:2]
            v0, vr = (acts[d][2], acts[d][3]) if has_vres else (None, None)
            th = jnp.tanh(xwa[rows, 0:LANES]).astype(BF16)
            xab = xwa[rows, LANES:2 * LANES].astype(BF16)
            vrb = vr[rows, :].astype(BF16) if has_vres else None
            for gi in range(gps):
                wave.append(_wkv_chain(d, gi, gps, rows, zrkv, th, xab, vrb, v0, ups, vecs, cbf, cf,
                                       q_refs.at[d * gps + gi], outs[d], outs[2 + d], has_vres, sub == 0))
        waves.append(wave)
    done = [[False] * len(wave) for wave in waves]
    held = [[False] * len(wave) for wave in waves]
    rounds = 0
    while not all(all(flags) for flags in done):
        for wi, wave in enumerate(waves):
            if rounds < wi * STAGE_LAG:
                continue
            for ci, chain in enumerate(wave):
                if done[wi][ci] or (held[wi][ci] and not done[wi - 1][ci]):
                    continue
                held[wi][ci] = False
                signal = next(chain, CHAIN_DONE)
                if signal is CHAIN_DONE:
                    done[wi][ci] = True
                elif signal is STATE_BARRIER:
                    held[wi][ci] = True
        rounds += 1


def _wkv(zs, zs_first, ups, vecs, batch, seq, d):
    t = zs.shape[0]
    has_vres = zs_first is not None
    c, gl = CHUNKS_PER_STEP * CHUNK, GROUP_LANES
    gps = _groups_per_step(d)
    w = gps * gl
    assert seq % c == 0 and d % w == 0
    nc = seq // c
    ng = d // w
    lora_pair0 = 3 * d // (2 * LANES)
    vres_block = (3 * d + 4 * LANES + gl) // LANES

    def dir_specs(rev):
        rw = (lambda b, i: b * nc + (nc - 1 - i)) if rev else (lambda b, i: b * nc + i)
        specs = [
            pl.BlockSpec((c, 3 * w), lambda b, g, i: (rw(b, i), g)),
            pl.BlockSpec((c, 2 * LANES), lambda b, g, i: (rw(b, i), lora_pair0 + rev)),
        ]
        args = [zs, zs]
        if has_vres:
            specs += [
                pl.BlockSpec((c, w), lambda b, g, i: (rw(b, i), 3 * g + 2)),
                pl.BlockSpec((c, LANES), lambda b, g, i: (rw(b, i), vres_block)),
            ]
            args += [zs_first, zs]
        return specs, args

    fs, fa = dir_specs(0)
    bs, ba = dir_specs(1)
    cbf, cf = _wkv_constants()
    p_specs = [
        pl.BlockSpec((5, LANES, w), lambda b, g, i: (0, 0, g)),
        pl.BlockSpec((8, w), lambda b, g, i: (0, g)),
        pl.BlockSpec(cbf.shape, lambda b, g, i: (0, 0, 0)),
        pl.BlockSpec(cf.shape, lambda b, g, i: (0, 0)),
    ]
    out_f = pl.BlockSpec((c, w), lambda b, g, i: (b * nc + i, g))
    out_b = pl.BlockSpec((c, w), lambda b, g, i: (b * nc + (nc - 1 - i), g))
    shp = jax.ShapeDtypeStruct((t, d), BF16)
    return pl.pallas_call(
        functools.partial(_wkv_kernel, has_vres=has_vres, gps=gps),
        out_shape=[shp, shp, shp, shp],
        grid=(batch, ng, nc),
        in_specs=fs + bs + p_specs,
        out_specs=[out_f, out_b, out_f, out_b],
        scratch_shapes=[pltpu.VMEM((2 * gps, gl, gl), F32)],
        compiler_params=_params("parallel", "parallel", "arbitrary"),
        name="wkv7_chunked",
    )(*fa, *ba, ups, vecs, cbf, cf)


def _rwkv_out_kernel(of_ref, ob_ref, bf_ref, bb_ref, xg_ref, gup_ref, lnw_ref, lnb_ref, bd_ref, o_ref):
    o = of_ref[...].astype(F32) + ob_ref[...].astype(F32)
    bd = bd_ref[...]
    inv_n = 1.0 / RWKV_HEAD
    tm = o.shape[0]

    def head_sum(x):
        s2 = _dot(jnp.concatenate(_split2(x), axis=0), bd)
        return s2[0:tm] + s2[tm:2 * tm]

    mu = head_sum(o) * inv_n
    oc = o - mu
    var = head_sum(oc * oc) * inv_n
    o_n = oc * lax.rsqrt(var + GN_EPS) * lnw_ref[...] + lnb_ref[...]
    gate = _dot(_sigmoid(xg_ref[...]).astype(BF16), gup_ref[...])
    bonus = bf_ref[...].astype(F32) + bb_ref[...].astype(F32)
    o_ref[...] = ((o_n + bonus) * gate).astype(o_ref.dtype)


def _rwkv_out(o_f, o_b, bon_f, bon_b, zs, gate_up, ln_w, ln_b, d):
    t = o_f.shape[0]
    gl = GROUP_LANES
    tm = _pick_tile(t, 2048, 8)
    xg_block = (3 * d + 4 * LANES) // gl
    bd = _wkv_constants()[0][2]
    big = pl.BlockSpec((tm, gl), lambda i, j: (i, j))
    return pl.pallas_call(
        _rwkv_out_kernel,
        out_shape=jax.ShapeDtypeStruct((t, d), BF16),
        grid=(t // tm, d // gl),
        in_specs=[
            big, big, big, big,
            pl.BlockSpec((tm, gl), lambda i, j: (i, xg_block)),
            pl.BlockSpec((gl, gl), lambda i, j: (0, j)),
            pl.BlockSpec((1, gl), lambda i, j: (0, j)),
            pl.BlockSpec((1, gl), lambda i, j: (0, j)),
            pl.BlockSpec((gl, gl), lambda i, j: (0, 0)),
        ],
        out_specs=big,
        compiler_params=_params("parallel", "parallel"),
        name="rwkv_out",
    )(o_f, o_b, bon_f, bon_b, zs, gate_up, ln_w, ln_b, bd)


def _attn_kernel(q_ref, k_ref, v_ref, cos_ref, sin_ref, lam_ref, sw_ref, o_ref, kr_ref, vt_ref, *,
                 tq, tqs, tk, lam_init):
    qi = pl.program_id(2)
    dh = DIFF_HEAD

    @pl.when(qi == 0)
    def _():
        blk = 512 if k_ref.shape[0] % 512 == 0 else k_ref.shape[0]
        for c0 in range(0, k_ref.shape[0], blk):
            cos = cos_ref[c0:c0 + blk, :]
            sin = sin_ref[c0:c0 + blk, :]
            for m in range(2):
                km = k_ref[c0:c0 + blk, m * dh:(m + 1) * dh].astype(F32)
                kr_ref[c0:c0 + blk, m * dh:(m + 1) * dh] = (
                    km * cos + pltpu.roll(km, dh // 2, 1) * sin).astype(BF16)
            vt_ref[:, c0:c0 + blk] = v_ref[c0:c0 + blk, :].astype(F32).T.astype(BF16)

    lam = (jnp.exp(jnp.sum(lam_ref[0:1, :] * lam_ref[1:2, :], axis=-1, keepdims=True))
           - jnp.exp(jnp.sum(lam_ref[2:3, :] * lam_ref[3:4, :], axis=-1, keepdims=True)) + lam_init)
    nkv = k_ref.shape[0] // tk
    for q0 in range(0, tq, tqs):
        row0 = pl.multiple_of(qi * tq + q0, tqs)
        cq = cos_ref[pl.ds(row0, tqs), :]
        sq = sin_ref[pl.ds(row0, tqs), :]
        qs = []
        for m in range(2):
            qm = q_ref[q0:q0 + tqs, m * dh:(m + 1) * dh].astype(F32)
            qm = (qm * cq + pltpu.roll(qm, dh // 2, 1) * sq) * (dh ** -0.5 * LOG2E)
            qs.append(qm.T.astype(BF16))
        mx, ls, acc = [None, None], [None, None], [None, None]
        for j in range(nkv):
            vt = vt_ref[:, j * tk:(j + 1) * tk]
            for m in range(2):
                st = _dot(kr_ref[j * tk:(j + 1) * tk, m * dh:(m + 1) * dh], qs[m])
                tile_max = jnp.max(st, axis=0, keepdims=True)
                m_new = tile_max if j == 0 else jnp.maximum(mx[m], tile_max)
                p = jnp.exp2(st - m_new)
                psum = jnp.sum(p, axis=0, keepdims=True)
                pv = _dot(vt, p.astype(BF16))
                if j == 0:
                    ls[m], acc[m] = psum, pv
                else:
                    alpha = jnp.exp2(mx[m] - m_new)
                    ls[m] = alpha * ls[m] + psum
                    acc[m] = alpha * acc[m] + pv
                mx[m] = m_new
        o = (acc[0] * (1.0 / ls[0]) - acc[1] * (lam / ls[1])).T
        o = o * lax.rsqrt(jnp.mean(o * o, axis=-1, keepdims=True) + SUBLN_EPS) * sw_ref[...]
        o_ref[q0:q0 + tqs, :] = (o * (1.0 - lam_init)).astype(o_ref.dtype)


def _diff_attention(z_att, cos, sin_signed, lam_vecs, subln_w, batch, seq, d, lam_init):
    t = z_att.shape[0]
    hw = 2 * DIFF_HEAD
    nh = d // hw
    tqs = _pick_tile(seq, MXU_DIM, 8)
    tq = _pick_tile(seq, 4 * tqs, tqs)
    nq = seq // tq
    tk = _pick_tile(seq, MXU_DIM, LANES)
    kern = functools.partial(_attn_kernel, tq=tq, tqs=tqs, tk=tk, lam_init=lam_init)
    return pl.pallas_call(
        kern,
        out_shape=jax.ShapeDtypeStruct((t, d), BF16),
        grid=(batch, nh, nq),
        in_specs=[
            pl.BlockSpec((tq, hw), lambda b, h, i: (b * nq + i, h)),
            pl.BlockSpec((seq, hw), lambda b, h, i: (b, nh + h)),
            pl.BlockSpec((seq, hw), lambda b, h, i: (b, 2 * nh + h)),
            pl.BlockSpec((seq, DIFF_HEAD), lambda b, h, i: (0, 0)),
            pl.BlockSpec((seq, DIFF_HEAD), lambda b, h, i: (0, 0)),
            pl.BlockSpec((4, DIFF_HEAD), lambda b, h, i: (0, 0)),
            pl.BlockSpec((1, hw), lambda b, h, i: (0, 0)),
        ],
        out_specs=pl.BlockSpec((tq, hw), lambda b, h, i: (b * nq + i, h)),
        scratch_shapes=[pltpu.VMEM((seq, hw), BF16), pltpu.VMEM((hw, seq), BF16)],
        compiler_params=_params("parallel", "parallel", "arbitrary"),
        name="diff_attention",
    )(z_att, z_att, z_att, cos, sin_signed, lam_vecs, subln_w)


def _merge_kernel(ya_ref, yb_ref, pa_ref, pb_ref, ga_ref, gb_ref, o_ref):
    ma = _dot(ya_ref[...], pa_ref[...])
    mb = _dot(yb_ref[...], pb_ref[...])
    ga = _sigmoid(ga_ref[...].astype(F32))
    gb = _sigmoid(gb_ref[...].astype(F32))
    o_ref[...] = (ga * ma + gb * mb).astype(o_ref.dtype)


def _merge(y_a, y_b, proj_a, proj_b, z_att, d):
    t = y_a.shape[0]
    tm = _pick_tile(t, 1024, 8)
    tn = _pick_tile(d, 1024, LANES)
    nb = d // tn
    return pl.pallas_call(
        _merge_kernel,
        out_shape=jax.ShapeDtypeStruct((t, d), BF16),
        grid=(t // tm, nb),
        in_specs=[
            pl.BlockSpec((tm, d), lambda i, j: (i, 0)),
            pl.BlockSpec((tm, d), lambda i, j: (i, 0)),
            pl.BlockSpec((d, tn), lambda i, j: (0, j)),
            pl.BlockSpec((d, tn), lambda i, j: (0, j)),
            pl.BlockSpec((tm, tn), lambda i, j: (i, 3 * nb + j)),
            pl.BlockSpec((tm, tn), lambda i, j: (i, 4 * nb + j)),
        ],
        out_specs=pl.BlockSpec((tm, tn), lambda i, j: (i, j)),
        compiler_params=_params("parallel", "arbitrary"),
        name="gated_merge",
    )(y_a, y_b, proj_a, proj_b, z_att, z_att)


def _mm_ln_kernel(a_ref, w_ref, x_ref, mod_ref, g_ref, b_ref, o_ref, *, nk, gate_row, alpha):
    kk = pl.program_id(1)
    tm = o_ref.shape[0]
    halves = [slice(0, tm // 2), slice(tm // 2, tm)]

    def layer_norm_rows(rows, acc):
        y = alpha * x_ref[rows, :] + mod_ref[gate_row:gate_row + 1, :] * acc
        mu = jnp.mean(y, axis=-1, keepdims=True)
        yc = y - mu
        var = jnp.mean(yc * yc, axis=-1, keepdims=True)
        o_ref[rows, :] = yc * lax.rsqrt(var + LN_EPS) * g_ref[...] + b_ref[...]

    def finish(accumulated):
        w = w_ref[...]
        acc0 = _dot(a_ref[halves[0], :], w)
        acc1 = _dot(a_ref[halves[1], :], w)
        if accumulated:
            acc0 = acc0 + o_ref[halves[0], :]
        layer_norm_rows(halves[0], acc0)
        if accumulated:
            acc1 = acc1 + o_ref[halves[1], :]
        layer_norm_rows(halves[1], acc1)

    if nk == 1:
        finish(False)
        return

    @pl.when(kk == 0)
    def _():
        o_ref[...] = _dot(a_ref[...], w_ref[...])

    @pl.when(jnp.logical_and(kk > 0, kk < nk - 1))
    def _():
        o_ref[...] += _dot(a_ref[...], w_ref[...])

    @pl.when(kk == nk - 1)
    def _():
        finish(True)


def _mm_residual_ln(a, w, x2, mod_l, gate_row, ln_g, ln_b, seq, alpha):
    t, kdim = a.shape
    d = w.shape[1]
    if kdim <= 2048:
        tm, tk = _pick_tile(seq, 512, 8), kdim
    else:
        tm, tk = _pick_tile(seq, 1024, 8), _pick_tile(kdim, 768, LANES)
    nk = kdim // tk
    tps = seq // tm
    kern = functools.partial(_mm_ln_kernel, nk=nk, gate_row=gate_row, alpha=alpha)
    return pl.pallas_call(
        kern,
        out_shape=jax.ShapeDtypeStruct((t, d), F32),
        grid=(t // tm, nk),
        in_specs=[
            pl.BlockSpec((tm, tk), lambda i, k: (i, k)),
            pl.BlockSpec((tk, d), lambda i, k: (k, 0)),
            pl.BlockSpec((tm, d), lambda i, k: (i, 0)),
            pl.BlockSpec((None, 6, d), lambda i, k: (i // tps, 0, 0)),
            pl.BlockSpec((1, d), lambda i, k: (0, 0)),
            pl.BlockSpec((1, d), lambda i, k: (0, 0)),
        ],
        out_specs=pl.BlockSpec((tm, d), lambda i, k: (i, 0)),
        compiler_params=_params("parallel", "arbitrary"),
        name="matmul_residual_ln",
    )(a, w, x2, mod_l, ln_g, ln_b)


def _ffn_up_kernel(x_ref, mod_ref, wg_ref, wu_ref, o_ref, u_ref, *, shift_row, scale_row):
    @pl.when(pl.program_id(1) == 0)
    def _():
        sc = 1.0 + mod_ref[scale_row:scale_row + 1, :]
        sh = mod_ref[shift_row:shift_row + 1, :]
        u_ref[...] = (x_ref[...] * sc + sh).astype(BF16)

    u = u_ref[...]
    half = o_ref.shape[1] // 2
    halves = [slice(0, half), slice(half, 2 * half)]
    prods = [(_dot(u, wg_ref[:, cols]), _dot(u, wu_ref[:, cols])) for cols in halves]
    for cols, (hg, hu) in zip(halves, prods):
        o_ref[:, cols] = (hg * _sigmoid(hg) * hu).astype(o_ref.dtype)


def _ffn_up(x2, mod_l, w_gate, w_up, seq):
    t, d = x2.shape
    f = w_gate.shape[1]
    tm = _pick_tile(seq, 1024, 8)
    tn = _pick_tile(f, 512, LANES)
    tps = seq // tm
    kern = functools.partial(_ffn_up_kernel, shift_row=3, scale_row=4)
    return pl.pallas_call(
        kern,
        out_shape=jax.ShapeDtypeStruct((t, f), BF16),
        grid=(t // tm, f // tn),
        in_specs=[
            pl.BlockSpec((tm, d), lambda i, j: (i, 0)),
            pl.BlockSpec((None, 6, d), lambda i, j: (i // tps, 0, 0)),
            pl.BlockSpec((d, tn), lambda i, j: (0, j)),
            pl.BlockSpec((d, tn), lambda i, j: (0, j)),
        ],
        out_specs=pl.BlockSpec((tm, tn), lambda i, j: (i, j)),
        scratch_shapes=[pltpu.VMEM((tm, d), BF16)],
        compiler_params=_params("parallel", "arbitrary"),
        name="ffn_up",
    )(x2, mod_l, w_gate, w_up)


def _pad_cols(a, width):
    return jnp.pad(a, [(0, 0)] * (a.ndim - 1) + [(0, width - a.shape[-1])])


def _pad_rows(a, height):
    return jnp.pad(a, [(0, 0)] * (a.ndim - 2) + [(0, height - a.shape[-2]), (0, 0)])


def _rope_tables(seq):
    pos = jnp.arange(seq, dtype=F32)
    inv = ROPE_THETA ** (-jnp.arange(0, DIFF_HEAD, 2, dtype=F32) / DIFF_HEAD)
    ang = pos[:, None] * inv[None, :]
    emb = jnp.concatenate([ang, ang], axis=-1)
    sign = jnp.where(jnp.arange(DIFF_HEAD) < DIFF_HEAD // 2, -1.0, 1.0).astype(F32)
    return jnp.cos(emb), jnp.sin(emb) * sign[None, :]


def kernel(x, c, ada_w, ada_b, w_in, shift_mu_prev, shift_mu_next, decay_w0, decay_up, iclr_a0, iclr_up, gate_up, k_k, k_a, r_k, ln_x_w, ln_x_b, vres_down, vres_up, vres_v0, lambda_q1, lambda_k1, lambda_q2, lambda_k2, subln_w, proj_a, proj_b, w_out, ln1_g, ln1_b, ffn_w_gate, ffn_w_up, ffn_w_down, ln2_g, ln2_b):
    batch, seq, d = x.shape
    depth = ada_w.shape[0]
    t = batch * seq
    dl, il, gl_rank = decay_up.shape[2], iclr_up.shape[2], gate_up.shape[1]
    vl = vres_down.shape[2]
    assert max(dl, il, vl) <= LANES and gl_rank <= GROUP_LANES
    rwkv_cols = 3 * d + 2 * dl + 2 * il + gl_rank
    alpha = (2 * depth) ** 0.25

    mod = _modulation(c, ada_w, ada_b)
    cos, sin_signed = _rope_tables(seq)
    x2 = x.reshape(t, d)
    zs_first = None

    for l in range(depth):
        mod_l = mod[l]
        w_l = w_in[l]
        gw = _groups_per_step(d) * GROUP_LANES
        cuts = np.cumsum([3 * d, dl, dl, il, il, gl_rank])

        def regroup(a):
            rkv, xw_f, xw_b, xa_f, xa_b, xg = jnp.split(a[..., :rwkv_cols], cuts[:-1], axis=-1)
            blocks = [rkv[..., p * d + g * gw:p * d + (g + 1) * gw] for g in range(d // gw) for p in range(3)]
            return blocks + [_pad_cols(xw_f, LANES), _pad_cols(xa_f, LANES), _pad_cols(xw_b, LANES),
                             _pad_cols(xa_b, LANES), _pad_cols(xg, GROUP_LANES)]

        w_parts = [p.astype(BF16) for p in regroup(w_l)]
        if l > 0:
            w_parts.append(_pad_cols(vres_down[l - 1].astype(BF16), LANES))
        else:
            w_parts.append(jnp.zeros((d, LANES), BF16))
        n_rwkv = sum(p.shape[1] for p in w_parts)
        n_pad = -(-n_rwkv // 1024) * 1024
        w_rwkv = _pad_cols(jnp.concatenate(w_parts, axis=1), n_pad)
        mu = jnp.stack([
            _pad_cols(jnp.concatenate(regroup(shift_mu_prev[l]), axis=0), n_pad),
            _pad_cols(jnp.concatenate(regroup(shift_mu_next[l]), axis=0), n_pad)])
        w_att = w_l[:, rwkv_cols:].astype(BF16)

        zs = _inproj_shift(x2, mod_l, w_rwkv, mu, seq, F32)
        z_att = _inproj_plain(x2, mod_l, w_att, seq, BF16)

        vup = _pad_rows(vres_up[l - 1], LANES) if l > 0 else jnp.zeros((LANES, d), F32)
        vv0 = vres_v0[l - 1] if l > 0 else jnp.zeros((d,), F32)
        ups = jnp.concatenate([_pad_rows(decay_up[l], LANES), _pad_rows(iclr_up[l], LANES), vup[None]],
                              axis=0).astype(BF16)
        vecs = jnp.stack([decay_w0[l, 0], decay_w0[l, 1], iclr_a0[l, 0], iclr_a0[l, 1],
                          k_k[l], k_a[l], r_k[l].reshape(d), vv0])
        o_f, o_b, bon_f, bon_b = _wkv(zs, zs_first if l > 0 else None, ups, vecs, batch, seq, d)
        if l == 0:
            zs_first = zs
        y_a = _rwkv_out(o_f, o_b, bon_f, bon_b, zs, _pad_rows(gate_up[l], GROUP_LANES).astype(BF16),
                        ln_x_w[l].reshape(1, d), ln_x_b[l].reshape(1, d), d)

        lam_init = 0.8 - 0.6 * math.exp(-0.3 * l)
        lam_vecs = jnp.stack([lambda_q1[l], lambda_k1[l], lambda_q2[l], lambda_k2[l]])
        y_b = _diff_attention(z_att, cos, sin_signed, lam_vecs, subln_w[l].reshape(1, -1),
                              batch, seq, d, lam_init)

        merged = _merge(y_a, y_b, proj_a[l].astype(BF16), proj_b[l].astype(BF16), z_att, d)
        x2 = _mm_residual_ln(merged, w_out[l].astype(BF16), x2, mod_l, 2,
                             ln1_g[l].reshape(1, d), ln1_b[l].reshape(1, d), seq, alpha)

        h = _ffn_up(x2, mod_l, ffn_w_gate[l].astype(BF16), ffn_w_up[l].astype(BF16), seq)
        x2 = _mm_residual_ln(h, ffn_w_down[l].astype(BF16), x2, mod_l, 5,
                             ln2_g[l].reshape(1, d), ln2_b[l].reshape(1, d), seq, alpha)

    return x2.reshape(batch, seq, d)
```

```python
import functools
import math

import numpy as np
import jax
import jax.numpy as jnp
from jax import lax
from jax.experimental import pallas as pl
from jax.experimental.pallas import tpu as pltpu

F32 = jnp.float32
BF16 = jnp.bfloat16
HIGHEST = lax.Precision.HIGHEST

RWKV_HEAD = 64
DIFF_HEAD = 128
ROPE_THETA = 10000.0
SUBLN_EPS = 1e-5
LN_EPS = 1e-5
GN_EPS = 1e-5 * RWKV_HEAD
EXP_NEG_HALF = math.exp(-0.5)
LOG2E = math.log2(math.e)

LANES = 128
MXU_DIM = 256
VMEM_LIMIT = 56 * 1024 * 1024

CHUNK = 64
HEADS_PER_GROUP = MXU_DIM // RWKV_HEAD
GROUP_LANES = HEADS_PER_GROUP * RWKV_HEAD
STACK_ROWS = HEADS_PER_GROUP * CHUNK
MAX_GROUPS_PER_STEP = 4
CHUNKS_PER_STEP = 2
STAGE_LAG = 4
STATE_BARRIER = "state-barrier"
CHAIN_DONE = "chain-done"
HALO = 16


def _sigmoid(x):
    return 1.0 / (1.0 + jnp.exp(-x))


def _dot(a, b, precision=None):
    return jnp.dot(a, b, preferred_element_type=F32, precision=precision)


def _dot_nt(a, b):
    return lax.dot_general(a, b, (((1,), (1,)), ((), ())), preferred_element_type=F32)


def _dot_tn(a, b):
    return lax.dot_general(a, b, (((0,), (0,)), ((), ())), preferred_element_type=F32)


def _split2(x):
    hi = x.astype(BF16)
    return hi, (x - hi.astype(F32)).astype(BF16)


def _params(*semantics):
    return pltpu.CompilerParams(dimension_semantics=semantics, vmem_limit_bytes=VMEM_LIMIT)


def _pick_tile(n, target, quantum):
    best = None
    t = quantum
    while t <= min(n, target):
        if n % t == 0:
            best = t
        t += quantum
    assert best is not None, (n, target, quantum)
    return best


def _mod_kernel(c_ref, w_ref, b_ref, o_ref):
    c = c_ref[...]
    c_act = c * _sigmoid(c)
    o_ref[...] = _dot(c_act, w_ref[...], HIGHEST) + b_ref[...]


def _modulation(c, ada_w, ada_b):
    nl, d, n6 = ada_w.shape
    b = c.shape[0]
    rows = -(-b // 8) * 8
    c_pad = jnp.pad(c, ((0, rows - b), (0, 0)))
    tn = _pick_tile(n6, 2048, LANES)
    out = pl.pallas_call(
        _mod_kernel,
        out_shape=jax.ShapeDtypeStruct((nl, rows, n6), F32),
        grid=(nl, n6 // tn),
        in_specs=[
            pl.BlockSpec((rows, d), lambda l, j: (0, 0)),
            pl.BlockSpec((None, d, tn), lambda l, j: (l, 0, j)),
            pl.BlockSpec((None, 1, tn), lambda l, j: (l, 0, j)),
        ],
        out_specs=pl.BlockSpec((None, rows, tn), lambda l, j: (l, 0, j)),
        compiler_params=_params("parallel", "parallel"),
        name="adaln_mod",
    )(c_pad, ada_w, ada_b.reshape(nl, 1, n6))
    return out[:, :b].reshape(nl, b, 6, d)


def _inproj_kernel(x_ref, xp_ref, xn_ref, mod_ref, w_ref, mu_ref, o_ref, u_ref, *acc_refs,
                   tm, tiles_per_seq, shift_row, scale_row):
    i = pl.program_id(0)
    j = pl.program_id(1)

    @pl.when(j == 0)
    def _():
        sc = 1.0 + mod_ref[scale_row:scale_row + 1, :]
        sh = mod_ref[shift_row:shift_row + 1, :]
        u_ref[0:HALO, :] = (xp_ref[...] * sc + sh).astype(BF16)
        u_ref[HALO:HALO + tm, :] = (x_ref[...] * sc + sh).astype(BF16)
        u_ref[HALO + tm:, :] = (xn_ref[...] * sc + sh).astype(BF16)

    row = lax.broadcasted_iota(jnp.int32, (tm, 1), 0)
    pos = i % tiles_per_seq
    first = jnp.logical_and(row == 0, pos == 0)
    last = jnp.logical_and(row == tm - 1, pos == tiles_per_seq - 1)

    def shift_out(acc_ref, cols):
        zc = acc_ref[HALO:HALO + tm, :]
        zp = jnp.where(first, 0.0, acc_ref[HALO - 1:HALO - 1 + tm, :])
        zn = jnp.where(last, 0.0, acc_ref[HALO + 1:HALO + 1 + tm, :])
        o_ref[:, cols] = (zc + mu_ref[0:1, cols] * (zp - zc) + mu_ref[1:2, cols] * (zn - zc)).astype(o_ref.dtype)

    sub = acc_refs[0].shape[1]
    parts = [slice(p * sub, (p + 1) * sub) for p in range(len(acc_refs))]
    for p, cols in enumerate(parts):
        acc_refs[p][...] = _dot(u_ref[...], w_ref[:, cols])
        if p > 0:
            shift_out(acc_refs[p - 1], parts[p - 1])
    shift_out(acc_refs[-1], parts[-1])


def _plain_inproj_kernel(x_ref, mod_ref, w_ref, o_ref, u_ref, *, shift_row, scale_row):
    @pl.when(pl.program_id(1) == 0)
    def _():
        sc = 1.0 + mod_ref[scale_row:scale_row + 1, :]
        sh = mod_ref[shift_row:shift_row + 1, :]
        u_ref[...] = (x_ref[...] * sc + sh).astype(BF16)

    o_ref[...] = _dot(u_ref[...], w_ref[...]).astype(o_ref.dtype)


def _inproj_shift(x2, mod_l, w, mu, seq, out_dtype):
    t, d = x2.shape
    n = w.shape[1]
    tm = _pick_tile(seq, 1024, HALO)
    tn = _pick_tile(n, 1024, MXU_DIM)
    sub = MXU_DIM
    tps = seq // tm
    hb = tm // HALO
    last_halo = t // HALO - 1
    kern = functools.partial(_inproj_kernel, tm=tm, tiles_per_seq=tps, shift_row=0, scale_row=1)
    return pl.pallas_call(
        kern,
        out_shape=jax.ShapeDtypeStruct((t, n), out_dtype),
        grid=(t // tm, n // tn),
        in_specs=[
            pl.BlockSpec((tm, d), lambda i, j: (i, 0)),
            pl.BlockSpec((HALO, d), lambda i, j: (jnp.maximum(i * hb - 1, 0), 0)),
            pl.BlockSpec((HALO, d), lambda i, j: (jnp.minimum((i + 1) * hb, last_halo), 0)),
            pl.BlockSpec((None, 6, d), lambda i, j: (i // tps, 0, 0)),
            pl.BlockSpec((d, tn), lambda i, j: (0, j)),
            pl.BlockSpec((2, tn), lambda i, j: (0, j)),
        ],
        out_specs=pl.BlockSpec((tm, tn), lambda i, j: (i, j)),
        scratch_shapes=[pltpu.VMEM((tm + 2 * HALO, d), BF16)]
        + [pltpu.VMEM((tm + 2 * HALO, sub), F32) for _ in range(tn // sub)],
        compiler_params=_params("parallel", "arbitrary"),
        name="inproj_shift",
    )(x2, x2, x2, mod_l, w, mu)


def _inproj_plain(x2, mod_l, w, seq, out_dtype):
    t, d = x2.shape
    n = w.shape[1]
    tm = _pick_tile(seq, 1024, HALO)
    tn = _pick_tile(n, 1024, LANES)
    tps = seq // tm
    kern = functools.partial(_plain_inproj_kernel, shift_row=0, scale_row=1)
    return pl.pallas_call(
        kern,
        out_shape=jax.ShapeDtypeStruct((t, n), out_dtype),
        grid=(t // tm, n // tn),
        in_specs=[
            pl.BlockSpec((tm, d), lambda i, j: (i, 0)),
            pl.BlockSpec((None, 6, d), lambda i, j: (i // tps, 0, 0)),
            pl.BlockSpec((d, tn), lambda i, j: (0, j)),
        ],
        out_specs=pl.BlockSpec((tm, tn), lambda i, j: (i, j)),
        scratch_shapes=[pltpu.VMEM((tm, d), BF16)],
        compiler_params=_params("parallel", "arbitrary"),
        name="inproj_plain",
    )(x2, mod_l, w)


def _wkv_constants():
    c, gc, gl = CHUNK, STACK_ROWS, GROUP_LANES
    assert gc == gl
    t = np.arange(c)
    lower = t[None, :] <= t[:, None]
    col_t = np.arange(gc) % c
    row_h = np.arange(gc) // c
    lane_h = np.arange(gl) // RWKV_HEAD
    tri = np.zeros((gc, gc), bool)
    tri[0:c, 0:c] = lower
    tri[c:2 * c, 0:c] = lower.T
    bd_lanes = lane_h[:, None] == lane_h[None, :]
    slab_h = np.stack([row_h[:, None] == lane_h[None, :], row_h[:, None] == row_h[None, :], bd_lanes, tri])
    slab_f = np.concatenate([
        col_t[None, :] < t[:, None], col_t[None, :] > t[:, None],
        col_t[None, :] <= t[:, None], col_t[None, :] >= t[:, None],
        col_t[None, :] == t[:, None], bd_lanes])
    return jnp.asarray(slab_h, BF16), jnp.asarray(slab_f, F32)


def _groups_per_step(d):
    groups = d // GROUP_LANES
    return max(g for g in range(1, MAX_GROUPS_PER_STEP + 1) if groups % g == 0)


def _wkv_chain(d, gi, gps, rows, zrkv, th, xab, vrb, v0, ups, vecs, cbf, cf, q_ref, o_ref, bon_ref,
               has_vres, state_ready):
    c, g, gc, gl = CHUNK, HEADS_PER_GROUP, STACK_ROWS, GROUP_LANES
    w = gps * gl
    sl = slice(gi * gl, (gi + 1) * gl)
    r = zrkv[rows, gi * gl:(gi + 1) * gl]
    k = zrkv[rows, w + gi * gl:w + (gi + 1) * gl]
    v = zrkv[rows, 2 * w + gi * gl:2 * w + (gi + 1) * gl]
    w0, a0 = vecs[d:d + 1, sl], vecs[2 + d:3 + d, sl]
    k_k, k_a, r_k = vecs[4:5, sl], vecs[5:6, sl], vecs[6:7, sl]
    sm, bdm, bdl = cbf[0], cbf[1], cbf[2]
    tri = cbf[3, d * c:(d + 1) * c, 0:c]
    ms = cf[d * c:(d + 1) * c, :] > 0.5
    mi = cf[(2 + d) * c:(3 + d) * c, :] > 0.5
    eye_w = cf[4 * c:5 * c, :]

    wl = _dot(th, ups[d, :, sl])
    al = _dot(xab, ups[2 + d, :, sl])
    if has_vres:
        ml = _dot(vrb, ups[4, :, sl])
    yield
    if has_vres:
        v = v + (v0[rows, sl] - v) * _sigmoid(vecs[7:8, sl] + ml)
    lw = -EXP_NEG_HALF * _sigmoid(w0 + wl)
    a = _sigmoid(a0 + al)
    kd = k * (1.0 + (a - 1.0) * k_a)
    kk0 = k * k_k
    sums = _dot(jnp.concatenate([*_split2(kk0 * kk0), *_split2(r * kd * r_k)], axis=0), bdl)
    cum2 = _dot(tri, jnp.concatenate(_split2(lw), axis=1))
    yield
    kk = kk0 * lax.rsqrt(jnp.maximum(sums[0:c] + sums[c:2 * c], 1e-24))
    bon_ref[rows, sl] = ((sums[2 * c:3 * c] + sums[3 * c:4 * c]) * v).astype(bon_ref.dtype)
    cum = cum2[:, 0:gl] + cum2[:, gl:2 * gl]
    tot = jnp.sum(lw, axis=0, keepdims=True)
    g_inv = jnp.exp(-cum)
    b = kk * a
    a_t = -(kk * jnp.exp(cum - lw))
    b_t = b * g_inv
    k_t = kd * g_inv
    r_t = r * jnp.exp(cum)
    g_rem = jnp.exp(tot - cum)
    bk_g = jnp.concatenate([b * g_rem, kd * g_rem], axis=0).astype(BF16)

    def stack(x):
        return jnp.concatenate([x.astype(BF16)] * g, axis=0) * sm

    def to_bd(xw):
        return jnp.concatenate([xw.astype(BF16)] * g, axis=0) * bdm

    ar = jnp.concatenate([a_t, r_t], axis=0).astype(BF16)
    nn = _dot_nt(ar, jnp.concatenate([stack(b_t), stack(k_t)], axis=0))
    if state_ready:
        q = q_ref[...]
        arq = _dot_nt(ar, q.astype(BF16))
    yield
    n_ab = jnp.where(ms, nn[0:c, 0:gc], 0.0)
    n_ak = jnp.where(ms, nn[0:c, gc:2 * gc], 0.0)
    n_rb = jnp.where(mi, nn[c:2 * c, 0:gc], 0.0).astype(BF16)
    n_rk = jnp.where(mi, nn[c:2 * c, gc:2 * gc], 0.0)
    pw = _dot(n_ab.astype(BF16), to_bd(n_ab))
    nv = _dot(jnp.concatenate([n_ak.astype(BF16), n_rk.astype(BF16)], axis=0), stack(v))
    yield
    inv = eye_w + n_ab
    steps = int(math.log2(c)) - 1
    for s in range(steps):
        rhs = to_bd(pw)
        if s < steps - 1:
            both = _dot(jnp.concatenate([pw.astype(BF16), inv.astype(BF16)], axis=0), rhs)
            yield
            pw = both[0:c]
            inv = inv + both[c:2 * c]
        else:
            last = _dot(inv.astype(BF16), rhs)
            yield
            inv = inv + last
    if not state_ready:
        yield STATE_BARRIER
        q = q_ref[...]
        arq = _dot_nt(ar, q.astype(BF16))
        yield
    y = arq[0:c] + nv[0:c]
    u = _dot(inv.astype(BF16), stack(y))
    yield
    o_u = _dot(n_rb, stack(u))
    upd = _dot_tn(jnp.concatenate([u, v], axis=0).astype(BF16), bk_g)
    yield
    o_ref[rows, sl] = (arq[c:2 * c] + nv[c:2 * c] + o_u).astype(o_ref.dtype)
    q_ref[...] = (q * jnp.exp(tot) + upd) * cf[5 * c:5 * c + gl, :]


def _wkv_kernel(*refs, has_vres, gps):
    n_act = 4 if has_vres else 2
    acts = [list(refs[0:n_act]), list(refs[n_act:2 * n_act])]
    ups, vecs, cbf, cf = refs[2 * n_act:2 * n_act + 4]
    outs = refs[2 * n_act + 4:2 * n_act + 8]
    q_refs = refs[2 * n_act + 8]

    @pl.when(pl.program_id(2) == 0)
    def _():
        q_refs[...] = jnp.zeros_like(q_refs)

    waves = []
    for sub in range(CHUNKS_PER_STEP):
        wave = []
        for d in range(2):
            pos = CHUNKS_PER_STEP - 1 - sub if d == 1 else sub
            rows = slice(pos * CHUNK, (pos + 1) * CHUNK)
            zrkv, xwa = acts[d][0:2]
            v0, vr = (acts[d][2], acts[d][3]) if has_vres else (None, None)
            th = jnp.tanh(xwa[rows, 0:LANES]).astype(BF16)
            xab = xwa[rows, LANES:2 * LANES].astype(BF16)
            vrb = vr[rows, :].astype(BF16) if has_vres else None
            for gi in range(gps):
                wave.append(_wkv_chain(d, gi, gps, rows, zrkv, th, xab, vrb, v0, ups, vecs, cbf, cf,
                                       q_refs.at[d * gps + gi], outs[d], outs[2 + d], has_vres, sub == 0))
        waves.append(wave)
    done = [[False] * len(wave) for wave in waves]
    held = [[False] * len(wave) for wave in waves]
    rounds = 0
    while not all(all(flags) for flags in done):
        for wi, wave in enumerate(waves):
            if rounds < wi * STAGE_LAG:
                continue
            for ci, chain in enumerate(wave):
                if done[wi][ci] or (held[wi][ci] and not done[wi - 1][ci]):
                    continue
                held[wi][ci] = False
                signal = next(chain, CHAIN_DONE)
                if signal is CHAIN_DONE:
                    done[wi][ci] = True
                elif signal is STATE_BARRIER:
                    held[wi][ci] = True
        rounds += 1


def _wkv(zs, zs_first, ups, vecs, batch, seq, d):
    t = zs.shape[0]
    has_vres = zs_first is not None
    c, gl = CHUNKS_PER_STEP * CHUNK, GROUP_LANES
    gps = _groups_per_step(d)
    w = gps * gl
    assert seq % c == 0 and d % w == 0
    nc = seq // c
    ng = d // w
    lora_pair0 = 3 * d // (2 * LANES)
    vres_block = (3 * d + 4 * LANES + gl) // LANES

    def dir_specs(rev):
        rw = (lambda b, i: b * nc + (nc - 1 - i)) if rev else (lambda b, i: b * nc + i)
        specs = [
            pl.BlockSpec((c, 3 * w), lambda b, g, i: (rw(b, i), g)),
            pl.BlockSpec((c, 2 * LANES), lambda b, g, i: (rw(b, i), lora_pair0 + rev)),
        ]
        args = [zs, zs]
        if has_vres:
            specs += [
                pl.BlockSpec((c, w), lambda b, g, i: (rw(b, i), 3 * g + 2)),
                pl.BlockSpec((c, LANES), lambda b, g, i: (rw(b, i), vres_block)),
            ]
            args += [zs_first, zs]
        return specs, args

    fs, fa = dir_specs(0)
    bs, ba = dir_specs(1)
    cbf, cf = _wkv_constants()
    p_specs = [
        pl.BlockSpec((5, LANES, w), lambda b, g, i: (0, 0, g)),
        pl.BlockSpec((8, w), lambda b, g, i: (0, g)),
        pl.BlockSpec(cbf.shape, lambda b, g, i: (0, 0, 0)),
        pl.BlockSpec(cf.shape, lambda b, g, i: (0, 0)),
    ]
    out_f = pl.BlockSpec((c, w), lambda b, g, i: (b * nc + i, g))
    out_b = pl.BlockSpec((c, w), lambda b, g, i: (b * nc + (nc - 1 - i), g))
    shp = jax.ShapeDtypeStruct((t, d), BF16)
    return pl.pallas_call(
        functools.partial(_wkv_kernel, has_vres=has_vres, gps=gps),
        out_shape=[shp, shp, shp, shp],
        grid=(batch, ng, nc),
        in_specs=fs + bs + p_specs,
        out_specs=[out_f, out_b, out_f, out_b],
        scratch_shapes=[pltpu.VMEM((2 * gps, gl, gl), F32)],
        compiler_params=_params("parallel", "parallel", "arbitrary"),
        name="wkv7_chunked",
    )(*fa, *ba, ups, vecs, cbf, cf)


def _rwkv_out_kernel(of_ref, ob_ref, bf_ref, bb_ref, xg_ref, gup_ref, lnw_ref, lnb_ref, bd_ref, o_ref):
    o = of_ref[...].astype(F32) + ob_ref[...].astype(F32)
    bd = bd_ref[...]
    inv_n = 1.0 / RWKV_HEAD
    tm = o.shape[0]

    def head_sum(x):
        s2 = _dot(jnp.concatenate(_split2(x), axis=0), bd)
        return s2[0:tm] + s2[tm:2 * tm]

    mu = head_sum(o) * inv_n
    oc = o - mu
    var = head_sum(oc * oc) * inv_n
    o_n = oc * lax.rsqrt(var + GN_EPS) * lnw_ref[...] + lnb_ref[...]
    gate = _dot(_sigmoid(xg_ref[...]).astype(BF16), gup_ref[...])
    bonus = bf_ref[...].astype(F32) + bb_ref[...].astype(F32)
    o_ref[...] = ((o_n + bonus) * gate).astype(o_ref.dtype)


def _rwkv_out(o_f, o_b, bon_f, bon_b, zs, gate_up, ln_w, ln_b, d):
    t = o_f.shape[0]
    gl = GROUP_LANES
    tm = _pick_tile(t, 2048, 8)
    xg_block = (3 * d + 4 * LANES) // gl
    bd = _wkv_constants()[0][2]
    big = pl.BlockSpec((tm, gl), lambda i, j: (i, j))
    return pl.pallas_call(
        _rwkv_out_kernel,
        out_shape=jax.ShapeDtypeStruct((t, d), BF16),
        grid=(t // tm, d // gl),
        in_specs=[
            big, big, big, big,
            pl.BlockSpec((tm, gl), lambda i, j: (i, xg_block)),
            pl.BlockSpec((gl, gl), lambda i, j: (0, j)),
            pl.BlockSpec((1, gl), lambda i, j: (0, j)),
            pl.BlockSpec((1, gl), lambda i, j: (0, j)),
            pl.BlockSpec((gl, gl), lambda i, j: (0, 0)),
        ],
        out_specs=big,
        compiler_params=_params("parallel", "parallel"),
        name="rwkv_out",
    )(o_f, o_b, bon_f, bon_b, zs, gate_up, ln_w, ln_b, bd)


def _attn_kernel(q_ref, k_ref, v_ref, cos_ref, sin_ref, lam_ref, sw_ref, o_ref, kr_ref, vt_ref, *,
                 tq, tqs, tk, lam_init):
    qi = pl.program_id(2)
    dh = DIFF_HEAD

    @pl.when(qi == 0)
    def _():
        blk = 512 if k_ref.shape[0] % 512 == 0 else k_ref.shape[0]
        for c0 in range(0, k_ref.shape[0], blk):
            cos = cos_ref[c0:c0 + blk, :]
            sin = sin_ref[c0:c0 + blk, :]
            for m in range(2):
                km = k_ref[c0:c0 + blk, m * dh:(m + 1) * dh].astype(F32)
                kr_ref[c0:c0 + blk, m * dh:(m + 1) * dh] = (
                    km * cos + pltpu.roll(km, dh // 2, 1) * sin).astype(BF16)
            vt_ref[:, c0:c0 + blk] = v_ref[c0:c0 + blk, :].astype(F32).T.astype(BF16)

    lam = (jnp.exp(jnp.sum(lam_ref[0:1, :] * lam_ref[1:2, :], axis=-1, keepdims=True))
           - jnp.exp(jnp.sum(lam_ref[2:3, :] * lam_ref[3:4, :], axis=-1, keepdims=True)) + lam_init)
    nkv = k_ref.shape[0] // tk
    for q0 in range(0, tq, tqs):
        row0 = pl.multiple_of(qi * tq + q0, tqs)
        cq = cos_ref[pl.ds(row0, tqs), :]
        sq = sin_ref[pl.ds(row0, tqs), :]
        qs = []
        for m in range(2):
            qm = q_ref[q0:q0 + tqs, m * dh:(m + 1) * dh].astype(F32)
            qm = (qm * cq + pltpu.roll(qm, dh // 2, 1) * sq) * (dh ** -0.5 * LOG2E)
            qs.append(qm.T.astype(BF16))
        mx, ls, acc = [None, None], [None, None], [None, None]
        for j in range(nkv):
            vt = vt_ref[:, j * tk:(j + 1) * tk]
            for m in range(2):
                st = _dot(kr_ref[j * tk:(j + 1) * tk, m * dh:(m + 1) * dh], qs[m])
                tile_max = jnp.max(st, axis=0, keepdims=True)
                m_new = tile_max if j == 0 else jnp.maximum(mx[m], tile_max)
                p = jnp.exp2(st - m_new)
                psum = jnp.sum(p, axis=0, keepdims=True)
                pv = _dot(vt, p.astype(BF16))
                if j == 0:
                    ls[m], acc[m] = psum, pv
                else:
                    alpha = jnp.exp2(mx[m] - m_new)
                    ls[m] = alpha * ls[m] + psum
                    acc[m] = alpha * acc[m] + pv
                mx[m] = m_new
        o = (acc[0] * (1.0 / ls[0]) - acc[1] * (lam / ls[1])).T
        o = o * lax.rsqrt(jnp.mean(o * o, axis=-1, keepdims=True) + SUBLN_EPS) * sw_ref[...]
        o_ref[q0:q0 + tqs, :] = (o * (1.0 - lam_init)).astype(o_ref.dtype)


def _diff_attention(z_att, cos, sin_signed, lam_vecs, subln_w, batch, seq, d, lam_init):
    t = z_att.shape[0]
    hw = 2 * DIFF_HEAD
    nh = d // hw
    tqs = _pick_tile(seq, MXU_DIM, 8)
    tq = _pick_tile(seq, 4 * tqs, tqs)
    nq = seq // tq
    tk = _pick_tile(seq, MXU_DIM, LANES)
    kern = functools.partial(_attn_kernel, tq=tq, tqs=tqs, tk=tk, lam_init=lam_init)
    return pl.pallas_call(
        kern,
        out_shape=jax.ShapeDtypeStruct((t, d), BF16),
        grid=(batch, nh, nq),
        in_specs=[
            pl.BlockSpec((tq, hw), lambda b, h, i: (b * nq + i, h)),
            pl.BlockSpec((seq, hw), lambda b, h, i: (b, nh + h)),
            pl.BlockSpec((seq, hw), lambda b, h, i: (b, 2 * nh + h)),
            pl.BlockSpec((seq, DIFF_HEAD), lambda b, h, i: (0, 0)),
            pl.BlockSpec((seq, DIFF_HEAD), lambda b, h, i: (0, 0)),
            pl.BlockSpec((4, DIFF_HEAD), lambda b, h, i: (0, 0)),
            pl.BlockSpec((1, hw), lambda b, h, i: (0, 0)),
        ],
        out_specs=pl.BlockSpec((tq, hw), lambda b, h, i: (b * nq + i, h)),
        scratch_shapes=[pltpu.VMEM((seq, hw), BF16), pltpu.VMEM((hw, seq), BF16)],
        compiler_params=_params("parallel", "parallel", "arbitrary"),
        name="diff_attention",
    )(z_att, z_att, z_att, cos, sin_signed, lam_vecs, subln_w)


def _merge_kernel(ya_ref, yb_ref, pa_ref, pb_ref, ga_ref, gb_ref, o_ref):
    ma = _dot(ya_ref[...], pa_ref[...])
    mb = _dot(yb_ref[...], pb_ref[...])
    ga = _sigmoid(ga_ref[...].astype(F32))
    gb = _sigmoid(gb_ref[...].astype(F32))
    o_ref[...] = (ga * ma + gb * mb).astype(o_ref.dtype)


def _merge(y_a, y_b, proj_a, proj_b, z_att, d):
    t = y_a.shape[0]
    tm = _pick_tile(t, 1024, 8)
    tn = _pick_tile(d, 1024, LANES)
    nb = d // tn
    return pl.pallas_call(
        _merge_kernel,
        out_shape=jax.ShapeDtypeStruct((t, d), BF16),
        grid=(t // tm, nb),
        in_specs=[
            pl.BlockSpec((tm, d), lambda i, j: (i, 0)),
            pl.BlockSpec((tm, d), lambda i, j: (i, 0)),
            pl.BlockSpec((d, tn), lambda i, j: (0, j)),
            pl.BlockSpec((d, tn), lambda i, j: (0, j)),
            pl.BlockSpec((tm, tn), lambda i, j: (i, 3 * nb + j)),
            pl.BlockSpec((tm, tn), lambda i, j: (i, 4 * nb + j)),
        ],
        out_specs=pl.BlockSpec((tm, tn), lambda i, j: (i, j)),
        compiler_params=_params("parallel", "arbitrary"),
        name="gated_merge",
    )(y_a, y_b, proj_a, proj_b, z_att, z_att)


def _mm_ln_kernel(a_ref, w_ref, x_ref, mod_ref, g_ref, b_ref, o_ref, *, nk, gate_row, alpha):
    kk = pl.program_id(1)
    tm = o_ref.shape[0]
    halves = [slice(0, tm // 2), slice(tm // 2, tm)]

    def layer_norm_rows(rows, acc):
        y = alpha * x_ref[rows, :] + mod_ref[gate_row:gate_row + 1, :] * acc
        mu = jnp.mean(y, axis=-1, keepdims=True)
        yc = y - mu
        var = jnp.mean(yc * yc, axis=-1, keepdims=True)
        o_ref[rows, :] = yc * lax.rsqrt(var + LN_EPS) * g_ref[...] + b_ref[...]

    def finish(accumulated):
        w = w_ref[...]
        acc0 = _dot(a_ref[halves[0], :], w)
        acc1 = _dot(a_ref[halves[1], :], w)
        if accumulated:
            acc0 = acc0 + o_ref[halves[0], :]
        layer_norm_rows(halves[0], acc0)
        if accumulated:
            acc1 = acc1 + o_ref[halves[1], :]
        layer_norm_rows(halves[1], acc1)

    if nk == 1:
        finish(False)
        return

    @pl.when(kk == 0)
    def _():
        o_ref[...] = _dot(a_ref[...], w_ref[...])

    @pl.when(jnp.logical_and(kk > 0, kk < nk - 1))
    def _():
        o_ref[...] += _dot(a_ref[...], w_ref[...])

    @pl.when(kk == nk - 1)
    def _():
        finish(True)


def _mm_residual_ln(a, w, x2, mod_l, gate_row, ln_g, ln_b, seq, alpha):
    t, kdim = a.shape
    d = w.shape[1]
    if kdim <= 2048:
        tm, tk = _pick_tile(seq, 512, 8), kdim
    else:
        tm, tk = _pick_tile(seq, 1024, 8), _pick_tile(kdim, 1408, LANES)
    nk = kdim // tk
    tps = seq // tm
    kern = functools.partial(_mm_ln_kernel, nk=nk, gate_row=gate_row, alpha=alpha)
    return pl.pallas_call(
        kern,
        out_shape=jax.ShapeDtypeStruct((t, d), F32),
        grid=(t // tm, nk),
        in_specs=[
            pl.BlockSpec((tm, tk), lambda i, k: (i, k)),
            pl.BlockSpec((tk, d), lambda i, k: (k, 0)),
            pl.BlockSpec((tm, d), lambda i, k: (i, 0), pipeline_mode=pl.Buffered(1)),
            pl.BlockSpec((None, 6, d), lambda i, k: (i // tps, 0, 0)),
            pl.BlockSpec((1, d), lambda i, k: (0, 0)),
            pl.BlockSpec((1, d), lambda i, k: (0, 0)),
        ],
        out_specs=pl.BlockSpec((tm, d), lambda i, k: (i, 0)),
        compiler_params=_params("parallel", "arbitrary"),
        name="matmul_residual_ln",
    )(a, w, x2, mod_l, ln_g, ln_b)


def _ffn_up_kernel(x_ref, mod_ref, wg_ref, wu_ref, o_ref, u_ref, *, shift_row, scale_row):
    @pl.when(pl.program_id(1) == 0)
    def _():
        sc = 1.0 + mod_ref[scale_row:scale_row + 1, :]
        sh = mod_ref[shift_row:shift_row + 1, :]
        u_ref[...] = (x_ref[...] * sc + sh).astype(BF16)

    u = u_ref[...]
    half = o_ref.shape[1] // 2
    halves = [slice(0, half), slice(half, 2 * half)]
    prods = [(_dot(u, wg_ref[:, cols]), _dot(u, wu_ref[:, cols])) for cols in halves]
    for cols, (hg, hu) in zip(halves, prods):
        o_ref[:, cols] = (hg * _sigmoid(hg) * hu).astype(o_ref.dtype)


def _ffn_up(x2, mod_l, w_gate, w_up, seq):
    t, d = x2.shape
    f = w_gate.shape[1]
    tm = _pick_tile(seq, 1024, 8)
    tn = _pick_tile(f, 512, LANES)
    tps = seq // tm
    kern = functools.partial(_ffn_up_kernel, shift_row=3, scale_row=4)
    return pl.pallas_call(
        kern,
        out_shape=jax.ShapeDtypeStruct((t, f), BF16),
        grid=(t // tm, f // tn),
        in_specs=[
            pl.BlockSpec((tm, d), lambda i, j: (i, 0)),
            pl.BlockSpec((None, 6, d), lambda i, j: (i // tps, 0, 0)),
            pl.BlockSpec((d, tn), lambda i, j: (0, j)),
            pl.BlockSpec((d, tn), lambda i, j: (0, j)),
        ],
        out_specs=pl.BlockSpec((tm, tn), lambda i, j: (i, j)),
        scratch_shapes=[pltpu.VMEM((tm, d), BF16)],
        compiler_params=_params("parallel", "arbitrary"),
        name="ffn_up",
    )(x2, mod_l, w_gate, w_up)


def _pad_cols(a, width):
    return jnp.pad(a, [(0, 0)] * (a.ndim - 1) + [(0, width - a.shape[-1])])


def _pad_rows(a, height):
    return jnp.pad(a, [(0, 0)] * (a.ndim - 2) + [(0, height - a.shape[-2]), (0, 0)])


def _rope_tables(seq):
    pos = jnp.arange(seq, dtype=F32)
    inv = ROPE_THETA ** (-jnp.arange(0, DIFF_HEAD, 2, dtype=F32) / DIFF_HEAD)
    ang = pos[:, None] * inv[None, :]
    emb = jnp.concatenate([ang, ang], axis=-1)
    sign = jnp.where(jnp.arange(DIFF_HEAD) < DIFF_HEAD // 2, -1.0, 1.0).astype(F32)
    return jnp.cos(emb), jnp.sin(emb) * sign[None, :]


def kernel(x, c, ada_w, ada_b, w_in, shift_mu_prev, shift_mu_next, decay_w0, decay_up, iclr_a0, iclr_up, gate_up, k_k, k_a, r_k, ln_x_w, ln_x_b, vres_down, vres_up, vres_v0, lambda_q1, lambda_k1, lambda_q2, lambda_k2, subln_w, proj_a, proj_b, w_out, ln1_g, ln1_b, ffn_w_gate, ffn_w_up, ffn_w_down, ln2_g, ln2_b):
    batch, seq, d = x.shape
    depth = ada_w.shape[0]
    t = batch * seq
    dl, il, gl_rank = decay_up.shape[2], iclr_up.shape[2], gate_up.shape[1]
    vl = vres_down.shape[2]
    assert max(dl, il, vl) <= LANES and gl_rank <= GROUP_LANES
    rwkv_cols = 3 * d + 2 * dl + 2 * il + gl_rank
    alpha = (2 * depth) ** 0.25

    mod = _modulation(c, ada_w, ada_b)
    cos, sin_signed = _rope_tables(seq)
    x2 = x.reshape(t, d)
    zs_first = None

    for l in range(depth):
        mod_l = mod[l]
        w_l = w_in[l]
        gw = _groups_per_step(d) * GROUP_LANES
        cuts = np.cumsum([3 * d, dl, dl, il, il, gl_rank])

        def regroup(a):
            rkv, xw_f, xw_b, xa_f, xa_b, xg = jnp.split(a[..., :rwkv_cols], cuts[:-1], axis=-1)
            blocks = [rkv[..., p * d + g * gw:p * d + (g + 1) * gw] for g in range(d // gw) for p in range(3)]
            return blocks + [_pad_cols(xw_f, LANES), _pad_cols(xa_f, LANES), _pad_cols(xw_b, LANES),
                             _pad_cols(xa_b, LANES), _pad_cols(xg, GROUP_LANES)]

        w_parts = [p.astype(BF16) for p in regroup(w_l)]
        if l > 0:
            w_parts.append(_pad_cols(vres_down[l - 1].astype(BF16), LANES))
        else:
            w_parts.append(jnp.zeros((d, LANES), BF16))
        n_rwkv = sum(p.shape[1] for p in w_parts)
        n_pad = -(-n_rwkv // 1024) * 1024
        w_rwkv = _pad_cols(jnp.concatenate(w_parts, axis=1), n_pad)
        mu = jnp.stack([
            _pad_cols(jnp.concatenate(regroup(shift_mu_prev[l]), axis=0), n_pad),
            _pad_cols(jnp.concatenate(regroup(shift_mu_next[l]), axis=0), n_pad)])
        w_att = w_l[:, rwkv_cols:].astype(BF16)

        zs = _inproj_shift(x2, mod_l, w_rwkv, mu, seq, F32)
        z_att = _inproj_plain(x2, mod_l, w_att, seq, BF16)

        vup = _pad_rows(vres_up[l - 1], LANES) if l > 0 else jnp.zeros((LANES, d), F32)
        vv0 = vres_v0[l - 1] if l > 0 else jnp.zeros((d,), F32)
        ups = jnp.concatenate([_pad_rows(decay_up[l], LANES), _pad_rows(iclr_up[l], LANES), vup[None]],
                              axis=0).astype(BF16)
        vecs = jnp.stack([decay_w0[l, 0], decay_w0[l, 1], iclr_a0[l, 0], iclr_a0[l, 1],
                          k_k[l], k_a[l], r_k[l].reshape(d), vv0])
        o_f, o_b, bon_f, bon_b = _wkv(zs, zs_first if l > 0 else None, ups, vecs, batch, seq, d)
        if l == 0:
            zs_first = zs
        y_a = _rwkv_out(o_f, o_b, bon_f, bon_b, zs, _pad_rows(gate_up[l], GROUP_LANES).astype(BF16),
                        ln_x_w[l].reshape(1, d), ln_x_b[l].reshape(1, d), d)

        lam_init = 0.8 - 0.6 * math.exp(-0.3 * l)
        lam_vecs = jnp.stack([lambda_q1[l], lambda_k1[l], lambda_q2[l], lambda_k2[l]])
        y_b = _diff_attention(z_att, cos, sin_signed, lam_vecs, subln_w[l].reshape(1, -1),
                              batch, seq, d, lam_init)

        merged = _merge(y_a, y_b, proj_a[l].astype(BF16), proj_b[l].astype(BF16), z_att, d)
        x2 = _mm_residual_ln(merged, w_out[l].astype(BF16), x2, mod_l, 2,
                             ln1_g[l].reshape(1, d), ln1_b[l].reshape(1, d), seq, alpha)

        h = _ffn_up(x2, mod_l, ffn_w_gate[l].astype(BF16), ffn_w_up[l].astype(BF16), seq)
        x2 = _mm_residual_ln(h, ffn_w_down[l].astype(BF16), x2, mod_l, 5,
                             ln2_g[l].reshape(1, d), ln2_b[l].reshape(1, d), seq, alpha)

    return x2.reshape(batch, seq, d)
```
